```python
import functools
import jax, jax.numpy as jnp
from jax import lax
import numpy as np

D_MODEL = 1024
BATCH = 8
SEQ = 2048
DEPTH = 2
DEC_BATCH = 32
DEC_SEQ = 8
PAST_LEN = 8192
PAGE_SIZE = 128

BRANCH_W = 512
POOL_WINDOWS = (2, 4, 8, 16)
N_POOL_GROUPS = len(POOL_WINDOWS)
POOL_GROUP_W = BRANCH_W // N_POOL_GROUPS
POOL_HIST = max(POOL_WINDOWS) - 1
N_HEADS = 8
HEAD_DIM = BRANCH_W // N_HEADS
MOBA_BLOCK = 256
MOBA_TOPK = 3
Q_CHUNK = 32
LRU_W = BRANCH_W
LRU_BLOCKS = 8
LRU_BLOCK_W = LRU_W // LRU_BLOCKS
CONV_W = 4
LRU_C = 8.0
D_FF_DENSE = 2816
N_EXPERTS = 8
TOP_K = 2
D_FF_EXPERT = 3584
N_DENSE_LAYERS = (DEPTH + 1) // 2
N_MOE_LAYERS = DEPTH // 2
D_IN = 6 * BRANCH_W + 3 * D_MODEL
EPS = 1e-6
NEG_INF = -1e30

kernel_name = "hybrid_pool_moba_rglru_decode_step"


def rms_norm(x, g):
    xf = x.astype(jnp.float32)
    y = xf * lax.rsqrt(jnp.mean(xf * xf, axis=-1, keepdims=True) + EPS)
    return (y * g.astype(jnp.float32)).astype(x.dtype)


def pool_mixer(xp, hist, start, w_group, scale):
    b, s, _ = xp.shape
    full = jnp.concatenate([hist.astype(xp.dtype), xp], axis=1)
    cs = jnp.cumsum(full.astype(jnp.float32), axis=1)
    cs = jnp.concatenate([jnp.zeros((b, 1, BRANCH_W), jnp.float32), cs], axis=1)
    pos = start + jnp.arange(s)
    upto = cs[:, POOL_HIST + 1:]
    groups = []
    for g, w in enumerate(POOL_WINDOWS):
        c0, c1 = g * POOL_GROUP_W, (g + 1) * POOL_GROUP_W
        lo = POOL_HIST + 1 - w
        win = upto[:, :, c0:c1] - cs[:, lo:lo + s, c0:c1]
        cnt = jnp.minimum(pos + 1, w).astype(jnp.float32)[None, :, None]
        groups.append(win / cnt - xp[:, :, c0:c1].astype(jnp.float32))
    pooled = jnp.stack(groups, axis=2).astype(xp.dtype)
    mixed = jnp.einsum('bsgc,gcd->bsgd', pooled, w_group).reshape(b, s, BRANCH_W)
    return mixed * scale, full[:, -POOL_HIST:]


def rglru_branch(x_lru, g_lru, conv_hist, h0, start, conv_w, conv_b, w_a, b_a, w_i, b_i, lam):
    b, s, _ = x_lru.shape
    full = jnp.concatenate([conv_hist.astype(x_lru.dtype), x_lru], axis=1)
    xc = sum(full[:, j:j + s] * conv_w[j] for j in range(CONV_W)) + conv_b
    xb = xc.reshape(b, s, LRU_BLOCKS, LRU_BLOCK_W)
    r = jax.nn.sigmoid((jnp.einsum('bshc,hcd->bshd', xb, w_a).reshape(b, s, LRU_W) + b_a).astype(jnp.float32))
    i = jax.nn.sigmoid((jnp.einsum('bshc,hcd->bshd', xb, w_i).reshape(b, s, LRU_W) + b_i).astype(jnp.float32))
    log_a = -LRU_C * r * jax.nn.softplus(-lam.astype(jnp.float32))
    a = jnp.exp(log_a)
    mult = jnp.sqrt(-jnp.expm1(2.0 * log_a))
    pos = start + jnp.arange(s)
    mult = jnp.where((pos == 0)[None, :, None], 1.0, mult)
    u = mult * i * xc.astype(jnp.float32)

    def step(h, au):
        a_t, u_t = au
        h = a_t * h + u_t
        return h, h

    h_last, hs = lax.scan(step, h0.astype(jnp.float32), (a.transpose(1, 0, 2), u.transpose(1, 0, 2)))
    y = hs.transpose(1, 0, 2).astype(x_lru.dtype) * jax.nn.gelu(g_lru)
    return y, full[:, -(CONV_W - 1):], h_last.astype(h0.dtype)


def moba_attend(q, k_sel, v_sel, sel_mask, k_own, v_own, own_mask):
    scale = HEAD_DIM ** -0.5
    s_own = jnp.einsum('bhqd,bhkd->bhqk', q, k_own, preferred_element_type=jnp.float32) * scale
    s_own = jnp.where(own_mask, s_own, NEG_INF)
    if k_sel is None:
        p = jax.nn.softmax(s_own, axis=-1).astype(v_own.dtype)
        return jnp.einsum('bhqk,bhkd->bhqd', p, v_own)
    s_sel = jnp.einsum('bhqd,bhqkd->bhqk', q, k_sel, preferred_element_type=jnp.float32) * scale
    if sel_mask is not None:
        s_sel = jnp.where(sel_mask, s_sel, NEG_INF)
    ls = s_sel.shape[-1]
    p = jax.nn.softmax(jnp.concatenate([s_sel, s_own], axis=-1), axis=-1).astype(v_own.dtype)
    return (jnp.einsum('bhqk,bhqkd->bhqd', p[..., :ls], v_sel)
            + jnp.einsum('bhqk,bhkd->bhqd', p[..., ls:], v_own))


def gather_blocks(kb, idx):
    g = jax.vmap(jax.vmap(lambda t, i: t[i]))(kb, idx)
    b, h, q, k, blk, d = g.shape
    return g.reshape(b, h, q, k * blk, d)


def moba_prompt(q, k, v):
    b, s, h, d = q.shape
    qh, kh, vh = (t.transpose(0, 2, 1, 3) for t in (q, k, v))
    n_full = s // MOBA_BLOCK
    k_sel = min(MOBA_TOPK, (s - 1) // MOBA_BLOCK)
    s_pad = -(-s // MOBA_BLOCK) * MOBA_BLOCK
    pad = ((0, 0), (0, 0), (0, s_pad - s), (0, 0))
    kb = jnp.pad(kh, pad).reshape(b, h, s_pad // MOBA_BLOCK, MOBA_BLOCK, d)
    vb = jnp.pad(vh, pad).reshape(b, h, s_pad // MOBA_BLOCK, MOBA_BLOCK, d)
    means = jnp.mean(kb[:, :, :n_full].astype(jnp.float32), axis=3)

    def chunk(c):
        q0 = c * Q_CHUNK
        qc = lax.dynamic_slice_in_dim(qh, q0, Q_CHUNK, axis=2)
        qpos = q0 + jnp.arange(Q_CHUNK)
        blk = q0 // MOBA_BLOCK
        k_own = lax.dynamic_index_in_dim(kb, blk, axis=2, keepdims=False)
        v_own = lax.dynamic_index_in_dim(vb, blk, axis=2, keepdims=False)
        kpos = blk * MOBA_BLOCK + jnp.arange(MOBA_BLOCK)
        own_mask = kpos[None, :] <= qpos[:, None]
        if k_sel == 0:
            return moba_attend(qc, None, None, None, k_own, v_own, own_mask)
        gs = jnp.einsum('bhqd,bhnd->bhqn', qc.astype(jnp.float32), means)
        gs = jnp.where(jnp.arange(n_full) < blk, gs, NEG_INF)
        _, idx = lax.top_k(gs, k_sel)
        sel_mask = jnp.repeat(idx < blk, MOBA_BLOCK, axis=-1)
        return moba_attend(qc, gather_blocks(kb, idx), gather_blocks(vb, idx), sel_mask,
                           k_own, v_own, own_mask)

    out = lax.map(chunk, jnp.arange(s // Q_CHUNK))
    return out.transpose(1, 0, 3, 2, 4).reshape(b, s, h * d)


def moba_sample(q, k_new, v_new, k_pool, v_pool, page_table):
    b, sq, h, d = q.shape
    n_pages = page_table.shape[1]
    past = n_pages * PAGE_SIZE
    ppb = MOBA_BLOCK // PAGE_SIZE
    n_full = past // MOBA_BLOCK
    own_first_page = n_full * ppb
    qh, kn, vn = (t.transpose(0, 2, 1, 3) for t in (q, k_new, v_new))
    if own_first_page < n_pages:
        pt_own = page_table[:, own_first_page:]
        ko_c = k_pool[pt_own].reshape(b, -1, h, d).transpose(0, 2, 1, 3)
        vo_c = v_pool[pt_own].reshape(b, -1, h, d).transpose(0, 2, 1, 3)
        k_own = jnp.concatenate([ko_c.astype(kn.dtype), kn], axis=2)
        v_own = jnp.concatenate([vo_c.astype(vn.dtype), vn], axis=2)
        n_c = ko_c.shape[2]
    else:
        k_own, v_own, n_c = kn, vn, 0
    own_mask = jnp.arange(n_c + sq)[None, :] <= (n_c + jnp.arange(sq))[:, None]
    k_sel = min(MOBA_TOPK, n_full)
    if k_sel == 0:
        out = moba_attend(qh, None, None, None, k_own, v_own, own_mask)
    else:
        k_past = k_pool[page_table[:, :own_first_page]]
        means = jnp.mean(k_past.reshape(b, n_full, MOBA_BLOCK, h, d).astype(jnp.float32), axis=2)
        gs = jnp.einsum('bhqd,bnhd->bhqn', qh.astype(jnp.float32), means)
        _, idx = lax.top_k(gs, k_sel)
        logical = idx[..., None] * ppb + jnp.arange(ppb)
        phys = jax.vmap(lambda pt, lg: pt[lg])(page_table, logical)
        heads = jnp.arange(h)[None, :, None, None, None, None]
        rows = jnp.arange(PAGE_SIZE)
        k_s = k_pool[phys[..., None], rows, heads].reshape(b, h, sq, k_sel * MOBA_BLOCK, d)
        v_s = v_pool[phys[..., None], rows, heads].reshape(b, h, sq, k_sel * MOBA_BLOCK, d)
        out = moba_attend(qh, k_s.astype(qh.dtype), v_s.astype(vn.dtype), None, k_own, v_own, own_mask)
    return out.transpose(0, 2, 1, 3).reshape(b, sq, h * d)


def mixer_sublayer(u, start, pool_hist, conv_hist, h0, attn_fn,
                   w_in, pool_w, pool_scale, conv_w, conv_b, lru_wa, lru_ba, lru_wi, lru_bi,
                   lru_lambda, proj_pool, proj_attn, proj_lru, w_o):
    b, s, _ = u.shape
    bw = BRANCH_W
    z = jnp.einsum('bsd,de->bse', u, w_in)
    x_pool = z[..., 0:bw]
    q = z[..., bw:2 * bw].reshape(b, s, N_HEADS, HEAD_DIM)
    k = z[..., 2 * bw:3 * bw].reshape(b, s, N_HEADS, HEAD_DIM)
    v = z[..., 3 * bw:4 * bw].reshape(b, s, N_HEADS, HEAD_DIM)
    x_lru = z[..., 4 * bw:5 * bw]
    g_lru = z[..., 5 * bw:6 * bw]
    gates = jax.nn.sigmoid(z[..., 6 * bw:].astype(jnp.float32)).reshape(b, s, 3, D_MODEL).astype(u.dtype)
    a_out, pool_new = pool_mixer(x_pool, pool_hist, start, pool_w, pool_scale)
    b_out = attn_fn(q, k, v)
    c_out, conv_new, h_new = rglru_branch(x_lru, g_lru, conv_hist, h0, start, conv_w, conv_b,
                                          lru_wa, lru_ba, lru_wi, lru_bi, lru_lambda)
    merged = (gates[:, :, 0] * (a_out @ proj_pool)
              + gates[:, :, 1] * (b_out @ proj_attn)
              + gates[:, :, 2] * (c_out @ proj_lru))
    return merged @ w_o, k, v, pool_new, conv_new, h_new


def swiglu(x, wg, wu, wd):
    return (jax.nn.silu(x @ wg) * (x @ wu)) @ wd


def moe_ffn(x, router_w, wg, wu, wd):
    logits = jnp.einsum('bsd,de->bse', x, router_w, preferred_element_type=jnp.float32)
    top_v, top_i = lax.top_k(logits, TOP_K)
    wts = jax.nn.softmax(top_v, axis=-1)
    gate = jnp.sum(jax.nn.one_hot(top_i, N_EXPERTS, dtype=jnp.float32) * wts[..., None], axis=-2)
    out = jnp.zeros(x.shape, jnp.float32)
    for e in range(N_EXPERTS):
        out = out + gate[..., e:e + 1] * swiglu(x, wg[e], wu[e], wd[e]).astype(jnp.float32)
    return out.astype(x.dtype)


def setup_inputs(seed: int = 0) -> dict:
    key = jax.random.key(seed)
    ks = jax.random.split(key, 32)
    f32 = jnp.float32
    n_pages = PAST_LEN // PAGE_SIZE
    n_used = DEC_BATCH * n_pages
    n_pool = n_used + n_used // 4
    perm = jax.random.permutation(ks[0], n_pool)
    page_table = perm[:n_used].reshape(DEC_BATCH, n_pages).astype(jnp.int32)

    def nrm(k, shape, fan_in):
        return jax.random.normal(k, shape, f32) * (fan_in ** -0.5)

    def gain(k, shape):
        return 1.0 + 0.05 * jax.random.normal(k, shape, f32)

    a_c = jax.random.uniform(ks[20], (DEPTH, LRU_W), f32, minval=0.9, maxval=0.999)
    a = a_c ** (1.0 / LRU_C)
    return {
        "x_prompt": jax.random.normal(ks[1], (BATCH, SEQ, D_MODEL), f32),
        "x_sample": jax.random.normal(ks[2], (DEC_BATCH, DEC_SEQ, D_MODEL), f32),
        "cache_k": jax.random.normal(ks[3], (DEPTH, n_pool, PAGE_SIZE, N_HEADS, HEAD_DIM), f32),
        "cache_v": jax.random.normal(ks[4], (DEPTH, n_pool, PAGE_SIZE, N_HEADS, HEAD_DIM), f32),
        "page_table": page_table,
        "state_pool": jax.random.normal(ks[5], (DEPTH, DEC_BATCH, POOL_HIST, BRANCH_W), f32),
        "state_conv": jax.random.normal(ks[6], (DEPTH, DEC_BATCH, CONV_W - 1, LRU_W), f32),
        "state_h": 0.5 * jax.random.normal(ks[7], (DEPTH, DEC_BATCH, LRU_W), f32),
        "norm_mix_pre": gain(ks[8], (DEPTH, D_MODEL)),
        "norm_mix_post": gain(ks[9], (DEPTH, D_MODEL)),
        "norm_ffn_pre": gain(ks[10], (DEPTH, D_MODEL)),
        "norm_ffn_post": gain(ks[11], (DEPTH, D_MODEL)),
        "w_in": nrm(ks[12], (DEPTH, D_MODEL, D_IN), D_MODEL),
        "pool_w": nrm(ks[13], (DEPTH, N_POOL_GROUPS, POOL_GROUP_W, POOL_GROUP_W), POOL_GROUP_W),
        "pool_scale": gain(ks[14], (DEPTH, BRANCH_W)),
        "conv_w": nrm(ks[15], (DEPTH, CONV_W, LRU_W), CONV_W),
        "conv_b": 0.01 * jax.random.normal(ks[16], (DEPTH, LRU_W), f32),
        "lru_wa": nrm(ks[17], (DEPTH, LRU_BLOCKS, LRU_BLOCK_W, LRU_BLOCK_W), LRU_BLOCK_W),
        "lru_ba": 0.01 * jax.random.normal(ks[18], (DEPTH, LRU_W), f32),
        "lru_wi": nrm(ks[19], (DEPTH, LRU_BLOCKS, LRU_BLOCK_W, LRU_BLOCK_W), LRU_BLOCK_W),
        "lru_bi": 0.01 * jax.random.normal(ks[21], (DEPTH, LRU_W), f32),
        "lru_lambda": jnp.log(a) - jnp.log1p(-a),
        "proj_pool": nrm(ks[22], (DEPTH, BRANCH_W, D_MODEL), BRANCH_W),
        "proj_attn": nrm(ks[23], (DEPTH, BRANCH_W, D_MODEL), BRANCH_W),
        "proj_lru": nrm(ks[24], (DEPTH, BRANCH_W, D_MODEL), BRANCH_W),
        "w_o": nrm(ks[25], (DEPTH, D_MODEL, D_MODEL), D_MODEL),
        "ffn_w_gate": nrm(ks[26], (N_DENSE_LAYERS, D_MODEL, D_FF_DENSE), D_MODEL),
        "ffn_w_up": nrm(ks[27], (N_DENSE_LAYERS, D_MODEL, D_FF_DENSE), D_MODEL),
        "ffn_w_down": nrm(ks[28], (N_DENSE_LAYERS, D_FF_DENSE, D_MODEL), D_FF_DENSE),
        "router_w": nrm(ks[29], (N_MOE_LAYERS, D_MODEL, N_EXPERTS), D_MODEL),
        "moe_w_gate": nrm(ks[30], (N_MOE_LAYERS, N_EXPERTS, D_MODEL, D_FF_EXPERT), D_MODEL),
        "moe_w_up": nrm(ks[31], (N_MOE_LAYERS, N_EXPERTS, D_MODEL, D_FF_EXPERT), D_MODEL),
        "moe_w_down": nrm(jax.random.fold_in(ks[31], 1), (N_MOE_LAYERS, N_EXPERTS, D_FF_EXPERT, D_MODEL), D_FF_EXPERT),
    }


def reference(x_prompt, x_sample, cache_k, cache_v, page_table, state_pool, state_conv, state_h,
              norm_mix_pre, norm_mix_post, norm_ffn_pre, norm_ffn_post, w_in, pool_w, pool_scale,
              conv_w, conv_b, lru_wa, lru_ba, lru_wi, lru_bi, lru_lambda, proj_pool, proj_attn,
              proj_lru, w_o, ffn_w_gate, ffn_w_up, ffn_w_down, router_w, moe_w_gate, moe_w_up,
              moe_w_down):
    past_len = page_table.shape[1] * PAGE_SIZE
    xp, xs = x_prompt, x_sample
    bp, sp, _ = xp.shape
    dt = xp.dtype
    zero_pool = jnp.zeros((bp, POOL_HIST, BRANCH_W), dt)
    zero_conv = jnp.zeros((bp, CONV_W - 1, LRU_W), dt)
    zero_h = jnp.zeros((bp, LRU_W), dt)
    kp_l, vp_l, ks_l, vs_l = [], [], [], []
    pp_l, ps_l, cp_l, cs_l, hp_l, hs_l = [], [], [], [], [], []
    for l in range(DEPTH):
        mix_w = (w_in[l], pool_w[l], pool_scale[l], conv_w[l], conv_b[l], lru_wa[l], lru_ba[l],
                 lru_wi[l], lru_bi[l], lru_lambda[l], proj_pool[l], proj_attn[l], proj_lru[l], w_o[l])
        m_p, k_p, v_p, pool_p, conv_p, h_p = mixer_sublayer(
            rms_norm(xp, norm_mix_pre[l]), 0, zero_pool, zero_conv, zero_h, moba_prompt, *mix_w)
        xp = xp + rms_norm(m_p, norm_mix_post[l])
        attn_s = functools.partial(moba_sample, k_pool=cache_k[l], v_pool=cache_v[l], page_table=page_table)
        m_s, k_s, v_s, pool_s, conv_s, h_s = mixer_sublayer(
            rms_norm(xs, norm_mix_pre[l]), past_len, state_pool[l], state_conv[l], state_h[l], attn_s, *mix_w)
        xs = xs + rms_norm(m_s, norm_mix_post[l])
        up, us = rms_norm(xp, norm_ffn_pre[l]), rms_norm(xs, norm_ffn_pre[l])
        if l % 2 == 0:
            j = l // 2
            f_p = swiglu(up, ffn_w_gate[j], ffn_w_up[j], ffn_w_down[j])
            f_s = swiglu(us, ffn_w_gate[j], ffn_w_up[j], ffn_w_down[j])
        else:
            j = l // 2
            f_p = moe_ffn(up, router_w[j], moe_w_gate[j], moe_w_up[j], moe_w_down[j])
            f_s = moe_ffn(us, router_w[j], moe_w_gate[j], moe_w_up[j], moe_w_down[j])
        xp = xp + rms_norm(f_p, norm_ffn_post[l])
        xs = xs + rms_norm(f_s, norm_ffn_post[l])
        kp_l.append(k_p.reshape(bp, sp // PAGE_SIZE, PAGE_SIZE, N_HEADS, HEAD_DIM))
        vp_l.append(v_p.reshape(bp, sp // PAGE_SIZE, PAGE_SIZE, N_HEADS, HEAD_DIM))
        ks_l.append(k_s)
        vs_l.append(v_s)
        pp_l.append(pool_p)
        ps_l.append(pool_s)
        cp_l.append(conv_p)
        cs_l.append(conv_s)
        hp_l.append(h_p)
        hs_l.append(h_s)
    new_k_prompt = jnp.stack(kp_l)
    new_v_prompt = jnp.stack(vp_l)
    new_k_sample = jnp.stack(ks_l)
    new_v_sample = jnp.stack(vs_l)
    new_pool_prompt = jnp.stack(pp_l)
    new_pool_sample = jnp.stack(ps_l)
    new_conv_prompt = jnp.stack(cp_l)
    new_conv_sample = jnp.stack(cs_l)
    new_h_prompt = jnp.stack(hp_l)
    new_h_sample = jnp.stack(hs_l)
    return (xp, xs, new_k_prompt, new_v_prompt, new_k_sample, new_v_sample,
            new_pool_prompt, new_pool_sample, new_conv_prompt, new_conv_sample,
            new_h_prompt, new_h_sample)
```

```python
import functools

import jax
import jax.numpy as jnp
from jax import lax
from jax.experimental import pallas as pl
from jax.experimental.pallas import tpu as pltpu

F32 = jnp.float32
BF16 = jnp.bfloat16

D_MODEL = 1024
BRANCH_W = 512
POOL_WINDOWS = (2, 4, 8, 16)
POOL_GROUP_W = BRANCH_W // len(POOL_WINDOWS)
POOL_HIST = max(POOL_WINDOWS) - 1
POOL_PAD = 16
N_HEADS = 8
HEAD_DIM = BRANCH_W // N_HEADS
MOBA_BLOCK = 256
MOBA_TOPK = 3
PAGE_SIZE = 128
LRU_BLOCKS = 8
LRU_BLOCK_W = BRANCH_W // LRU_BLOCKS
CONV_W = 4
CONV_PAD = 8
LRU_C = 8.0
N_EXPERTS = 8
TOP_K = 2
EPS = 1e-6
NEG_INF = -1e30
D_QKV = 6 * BRANCH_W
SUBLANES = 8
LANES = 128
VMEM_LIMIT = 56 * 1024 * 1024


def _cparams(n_axes):
    return pltpu.CompilerParams(
        dimension_semantics=("arbitrary",) * n_axes, vmem_limit_bytes=VMEM_LIMIT)


def _rms(x, g):
    return x * lax.rsqrt(jnp.mean(x * x, axis=-1, keepdims=True) + EPS) * g


def _split_bf16(x):
    hi = x.astype(BF16)
    lo = (x - hi.astype(F32)).astype(BF16)
    return hi, lo


def _dot_split(a, b, dims):
    a_hi, a_lo = _split_bf16(a)
    b_hi, b_lo = _split_bf16(b)
    d = functools.partial(lax.dot_general, dimension_numbers=dims, preferred_element_type=F32)
    return d(a_hi, b_hi) + (d(a_hi, b_lo) + d(a_lo, b_hi))


_NN = (((1,), (0,)), ((), ()))
_NT = (((1,), (1,)), ((), ()))


def _softplus(x):
    return jnp.maximum(x, 0.0) + jnp.log1p(jnp.exp(-jnp.abs(x)))


def _lru_coeffs(xc, wbd_ref, ba, bi, lam, first_row_is_start):
    xcb = xc.astype(BF16)
    half = BRANCH_W // 2
    g0 = jnp.dot(xcb[:, :half], wbd_ref[0], preferred_element_type=F32)
    g1 = jnp.dot(xcb[:, half:], wbd_ref[1], preferred_element_type=F32)
    pre_a = jnp.concatenate([g0[:, :half], g1[:, :half]], axis=1)
    pre_i = jnp.concatenate([g0[:, half:], g1[:, half:]], axis=1)
    r = jax.nn.sigmoid(pre_a + ba)
    i = jax.nn.sigmoid(pre_i + bi)
    log_a = (-LRU_C) * r * _softplus(-lam)
    a = jnp.exp(log_a)
    t = jnp.tanh(log_a)
    mult = jnp.sqrt(-2.0 * t / (1.0 - t))
    if first_row_is_start is not None:
        mult = jnp.where(first_row_is_start, 1.0, mult)
    return a, mult * i * xc


def _scan8(a, u):
    r8 = lax.broadcasted_iota(jnp.int32, a.shape, 0) & (SUBLANES - 1)
    for d in (1, 2, 4):
        a_s = pltpu.roll(a, d, axis=0)
        u_s = pltpu.roll(u, d, axis=0)
        m = r8 >= d
        u = jnp.where(m, u + a * u_s, u)
        a = jnp.where(m, a * a_s, a)
    return a, u


def _pool_mix(win_groups, xp, cnt_groups, poolw_ref, pscale):
    outs = []
    for g in range(len(POOL_WINDOWS)):
        c0 = g * POOL_GROUP_W
        pooled = win_groups[g] / cnt_groups[g] - xp[:, c0:c0 + POOL_GROUP_W]
        outs.append(jnp.dot(pooled.astype(BF16), poolw_ref[g], preferred_element_type=F32))
    return jnp.concatenate(outs, axis=1) * pscale


def _mixer_in_prompt_kernel(x_ref, gpre_ref, win_ref, poolw_ref, pscale_ref, convw_ref, convb_ref,
                            wbd_ref, ba_ref, bi_ref, lam_ref,
                            q_ref, k_ref, v_ref, a_ref, c_ref, ptail_ref, ctail_ref, hlast_ref,
                            pbuf, cbuf, hcar, *, ts):
    s = pl.program_id(1)

    @pl.when(s == 0)
    def _():
        pbuf[0:POOL_PAD, :] = jnp.zeros((POOL_PAD, BRANCH_W), F32)
        cbuf[0:CONV_PAD, :] = jnp.zeros((CONV_PAD, BRANCH_W), F32)
        hcar[...] = jnp.zeros((1, BRANCH_W), F32)

    u = _rms(x_ref[...], gpre_ref[...])
    z = jnp.dot(u.astype(BF16), win_ref[...], preferred_element_type=F32)
    bw = BRANCH_W
    xp = z[:, 0:bw]
    q_ref[...] = z[:, bw:2 * bw]
    k_ref[...] = z[:, 2 * bw:3 * bw]
    v_ref[...] = z[:, 3 * bw:4 * bw]
    xl = z[:, 4 * bw:5 * bw]
    gl = z[:, 5 * bw:6 * bw]

    pos = s * ts + lax.broadcasted_iota(jnp.int32, (ts, 1), 0)

    pbuf[POOL_PAD:POOL_PAD + ts, :] = xp
    wins, cnts = [], []
    for g, w in enumerate(POOL_WINDOWS):
        c0 = g * POOL_GROUP_W
        win = pbuf[POOL_PAD:POOL_PAD + ts, c0:c0 + POOL_GROUP_W]
        for j in range(1, w):
            win = win + pbuf[POOL_PAD - j:POOL_PAD - j + ts, c0:c0 + POOL_GROUP_W]
        wins.append(win)
        cnts.append(jnp.minimum(pos + 1, w).astype(F32))
    a_ref[...] = _pool_mix(wins, xp, cnts, poolw_ref, pscale_ref[...]).astype(a_ref.dtype)
    tail = pbuf[ts:ts + POOL_PAD, :]
    pbuf[0:POOL_PAD, :] = tail
    ptail_ref[...] = tail

    cbuf[CONV_PAD:CONV_PAD + ts, :] = xl
    xc = convb_ref[...] + xl * convw_ref[CONV_W - 1:CONV_W, :]
    for j in range(CONV_W - 1):
        off = CONV_PAD - (CONV_W - 1) + j
        xc = xc + cbuf[off:off + ts, :] * convw_ref[j:j + 1, :]
    ctail = cbuf[ts:ts + CONV_PAD, :]
    cbuf[0:CONV_PAD, :] = ctail
    ctail_ref[...] = ctail

    a, uu = _lru_coeffs(xc, wbd_ref, ba_ref[...], bi_ref[...], lam_ref[...], pos == 0)
    a, uu = _scan8(a, uu)
    carry = jnp.broadcast_to(hcar[...], (SUBLANES, BRANCH_W))
    hs = []
    for g in range(ts // SUBLANES):
        hg = a[g * SUBLANES:(g + 1) * SUBLANES] * carry + uu[g * SUBLANES:(g + 1) * SUBLANES]
        hs.append(hg)
        carry = jnp.broadcast_to(hg[SUBLANES - 1:SUBLANES, :], (SUBLANES, BRANCH_W))
    h = jnp.concatenate(hs, axis=0)
    hcar[...] = carry[0:1, :]
    hlast_ref[...] = carry[0:1, :]
    c_ref[...] = (h * jax.nn.gelu(gl)).astype(c_ref.dtype)


def _const_spec(shape):
    nd = len(shape)
    return pl.BlockSpec(shape, lambda *_: (0,) * nd)


def _mixer_in_prompt(x, n_batch, seq, ts, gpre, win, poolw, pscale, convw, convb, wbd, ba, bi, lam):
    n = n_batch * seq
    nt = seq // ts
    row = lambda w: pl.BlockSpec((ts, w), lambda b, s: (b * nt + s, 0))
    per_b = lambda r: pl.BlockSpec((None, r, BRANCH_W), lambda b, s: (b, 0, 0))
    consts = (gpre, win, poolw, pscale, convw, convb, wbd, ba, bi, lam)
    return pl.pallas_call(
        functools.partial(_mixer_in_prompt_kernel, ts=ts),
        grid=(n_batch, nt),
        in_specs=[row(D_MODEL)] + [_const_spec(c.shape) for c in consts],
        out_specs=[row(BRANCH_W)] * 5 + [per_b(POOL_PAD), per_b(CONV_PAD), per_b(1)],
        out_shape=[jax.ShapeDtypeStruct((n, BRANCH_W), F32)] * 3
        + [jax.ShapeDtypeStruct((n, BRANCH_W), BF16)] * 2
        + [jax.ShapeDtypeStruct((n_batch, POOL_PAD, BRANCH_W), F32),
           jax.ShapeDtypeStruct((n_batch, CONV_PAD, BRANCH_W), F32),
           jax.ShapeDtypeStruct((n_batch, 1, BRANCH_W), F32)],
        scratch_shapes=[pltpu.VMEM((POOL_PAD + ts, BRANCH_W), F32),
                        pltpu.VMEM((CONV_PAD + ts, BRANCH_W), F32),
                        pltpu.VMEM((1, BRANCH_W), F32)],
        compiler_params=_cparams(2),
        name="mixer_in_prompt",
    )(x, *consts)


def _mixer_in_sample_kernel(x_ref, phist_ref, chist_ref, h0_ref, gpre_ref, win_ref, poolw_ref,
                            pscale_ref, convw_ref, convb_ref, wbd_ref, ba_ref, bi_ref, lam_ref,
                            q_ref, k_ref, v_ref, a_ref, c_ref, ptail_ref, ctail_ref, hlast_ref,
                            pbuf, cbuf, *, n_batch, sq, start):
    rows = n_batch * sq
    u = _rms(x_ref[...], gpre_ref[...])
    z = jnp.dot(u.astype(BF16), win_ref[...], preferred_element_type=F32)
    bw = BRANCH_W
    xp = z[:, 0:bw]
    q_ref[...] = z[:, bw:2 * bw]
    k_ref[...] = z[:, 2 * bw:3 * bw]
    v_ref[...] = z[:, 3 * bw:4 * bw]
    xl = z[:, 4 * bw:5 * bw]
    gl = z[:, 5 * bw:6 * bw]

    pos = start + (lax.broadcasted_iota(jnp.int32, (rows, 1), 0) & (sq - 1))

    pbuf[:, 0:POOL_PAD, :] = phist_ref[...]
    pbuf[:, POOL_PAD:POOL_PAD + sq, :] = xp.reshape(n_batch, sq, bw)
    wins, cnts = [], []
    for g, w in enumerate(POOL_WINDOWS):
        c0 = g * POOL_GROUP_W
        win = pbuf[:, POOL_PAD:POOL_PAD + sq, c0:c0 + POOL_GROUP_W]
        for j in range(1, w):
            win = win + pbuf[:, POOL_PAD - j:POOL_PAD - j + sq, c0:c0 + POOL_GROUP_W]
        wins.append(win.reshape(rows, POOL_GROUP_W))
        cnts.append(jnp.minimum(pos + 1, w).astype(F32))
    a_ref[...] = _pool_mix(wins, xp, cnts, poolw_ref, pscale_ref[...]).astype(a_ref.dtype)
    ptail_ref[...] = pbuf[:, sq:sq + POOL_PAD, :]

    cbuf[:, 0:CONV_PAD, :] = chist_ref[...]
    cbuf[:, CONV_PAD:CONV_PAD + sq, :] = xl.reshape(n_batch, sq, bw)
    xc = xl.reshape(n_batch, sq, bw) * convw_ref[CONV_W - 1:CONV_W, :]
    for j in range(CONV_W - 1):
        off = CONV_PAD - (CONV_W - 1) + j
        xc = xc + cbuf[:, off:off + sq, :] * convw_ref[j:j + 1, :]
    xc = xc.reshape(rows, bw) + convb_ref[...]
    ctail_ref[...] = cbuf[:, sq:sq + CONV_PAD, :]

    a, uu = _lru_coeffs(xc, wbd_ref, ba_ref[...], bi_ref[...], lam_ref[...],
                        (pos == 0) if start == 0 else None)
    a, uu = _scan8(a, uu)
    h = a.reshape(n_batch, sq, bw) * h0_ref[...] + uu.reshape(n_batch, sq, bw)
    hlast_ref[...] = h[:, sq - 1:sq, :]
    c_ref[...] = (h.reshape(rows, bw) * jax.nn.gelu(gl)).astype(c_ref.dtype)


def _mixer_in_sample(x, n_batch, sq, start, phist, chist, h0, gpre, win, poolw, pscale, convw, convb,
                     wbd, ba, bi, lam):
    assert sq == SUBLANES, "sample sequences must fill exactly one sublane group"
    rows = n_batch * sq
    args = (x, phist, chist, h0, gpre, win, poolw, pscale, convw, convb, wbd, ba, bi, lam)
    return pl.pallas_call(
        functools.partial(_mixer_in_sample_kernel, n_batch=n_batch, sq=sq, start=start),
        grid=(1,),
        in_specs=[_const_spec(a.shape) for a in args],
        out_specs=[_const_spec((rows, BRANCH_W))] * 5
        + [_const_spec((n_batch, POOL_PAD, BRANCH_W)), _const_spec((n_batch, CONV_PAD, BRANCH_W)),
           _const_spec((n_batch, 1, BRANCH_W))],
        out_shape=[jax.ShapeDtypeStruct((rows, BRANCH_W), F32)] * 3
        + [jax.ShapeDtypeStruct((rows, BRANCH_W), BF16)] * 2
        + [jax.ShapeDtypeStruct((n_batch, POOL_PAD, BRANCH_W), F32),
           jax.ShapeDtypeStruct((n_batch, CONV_PAD, BRANCH_W), F32),
           jax.ShapeDtypeStruct((n_batch, 1, BRANCH_W), F32)],
        scratch_shapes=[pltpu.VMEM((n_batch, POOL_PAD + sq, BRANCH_W), F32),
                        pltpu.VMEM((n_batch, CONV_PAD + sq, BRANCH_W), F32)],
        compiler_params=_cparams(1),
        name="mixer_in_sample",
    )(*args)


def _topk_mask_rows(g, n_valid, k):
    n = g.shape[0]
    ridx = lax.broadcasted_iota(jnp.int32, g.shape, 0)
    valid = ridx < n_valid
    rows = []
    for j in range(n):
        gj = g[j:j + 1, :]
        beats = jnp.where(ridx < j, jnp.where(g >= gj, 1.0, 0.0), jnp.where(g > gj, 1.0, 0.0))
        cnt = jnp.sum(jnp.where(valid, beats, 0.0), axis=0, keepdims=True)
        rows.append(cnt)
    return jnp.concatenate(rows, axis=0) < float(k)


def _moba_prompt_kernel(q_ref, k_ref, v_ref, o_ref, kbf, vtb, means_sc, sel_sc, *, nblk):
    i = pl.program_id(2)
    blk = MOBA_BLOCK
    pair_w = 2 * HEAD_DIM

    @pl.when(i == 0)
    def _():
        if nblk < SUBLANES:
            means_sc[nblk:SUBLANES, :] = jnp.zeros((SUBLANES - nblk, pair_w), F32)
        for n in range(nblk):
            kb = k_ref[n * blk:(n + 1) * blk, :]
            means_sc[n:n + 1, :] = jnp.sum(kb, axis=0, keepdims=True) * (1.0 / blk)
            kbf[n] = kb.astype(BF16)
            vtb[n] = v_ref[n * blk:(n + 1) * blk, :].T.astype(BF16)

    qt = q_ref[...].T
    ch = lax.broadcasted_iota(jnp.int32, (pair_w, 1), 0)
    means = means_sc[...]
    key_r = lax.broadcasted_iota(jnp.int32, (blk, blk), 0)
    qry_c = lax.broadcasted_iota(jnp.int32, (blk, blk), 1)
    causal = key_r <= qry_c
    scale = HEAD_DIM ** -0.5

    outs = []
    for h in range(2):
        qh = jnp.where((ch >= h * HEAD_DIM) & (ch < (h + 1) * HEAD_DIM), qt, 0.0)
        gs = _dot_split(means, qh, _NN)
        sel_sc[...] = jnp.where(_topk_mask_rows(gs, i, MOBA_TOPK), 1.0, 0.0)
        qs = (qh * scale).astype(BF16)

        st = jnp.dot(kbf[i], qs, preferred_element_type=F32)
        st = jnp.where(causal, st, NEG_INF)
        m = jnp.max(st, axis=0, keepdims=True)
        p = jnp.exp(st - m)
        l = jnp.sum(p, axis=0, keepdims=True)
        acc = jnp.dot(vtb[i], p.astype(BF16), preferred_element_type=F32)

        def past_block(j, carry):
            m, l, acc = carry
            st = jnp.dot(kbf[j], qs, preferred_element_type=F32)
            st = jnp.where(sel_sc[pl.ds(j, 1), :] > 0.5, st, NEG_INF)
            m_new = jnp.maximum(m, jnp.max(st, axis=0, keepdims=True))
            alpha = jnp.exp(m - m_new)
            p = jnp.exp(st - m_new)
            l = alpha * l + jnp.sum(p, axis=0, keepdims=True)
            acc = alpha * acc + jnp.dot(vtb[j], p.astype(BF16), preferred_element_type=F32)
            return m_new, l, acc

        m, l, acc = lax.fori_loop(0, i, past_block, (m, l, acc))
        outs.append(acc / l)

    out_t = jnp.where(ch < HEAD_DIM, outs[0], outs[1])
    o_ref[...] = out_t.T.astype(o_ref.dtype)


def _moba_prompt(q, k, v, n_batch, seq):
    assert seq % MOBA_BLOCK == 0
    nblk = seq // MOBA_BLOCK
    assert nblk <= SUBLANES
    pair_w = 2 * HEAD_DIM
    n_pairs = BRANCH_W // pair_w
    q3, k3, v3 = (t.reshape(n_batch, seq, BRANCH_W) for t in (q, k, v))
    qspec = pl.BlockSpec((None, MOBA_BLOCK, pair_w), lambda b, hp, i: (b, i, hp))
    kvspec = pl.BlockSpec((None, seq, pair_w), lambda b, hp, i: (b, 0, hp))
    out = pl.pallas_call(
        functools.partial(_moba_prompt_kernel, nblk=nblk),
        grid=(n_batch, n_pairs, nblk),
        in_specs=[qspec, kvspec, kvspec],
        out_specs=qspec,
        out_shape=jax.ShapeDtypeStruct((n_batch, seq, BRANCH_W), BF16),
        scratch_shapes=[pltpu.VMEM((nblk, MOBA_BLOCK, pair_w), BF16),
                        pltpu.VMEM((nblk, pair_w, MOBA_BLOCK), BF16),
                        pltpu.VMEM((SUBLANES, pair_w), F32),
                        pltpu.VMEM((SUBLANES, MOBA_BLOCK), F32)],
        compiler_params=_cparams(3),
        name="moba_prompt",
    )(q3, k3, v3)
    return out.reshape(n_batch * seq, BRANCH_W)


CHUNK_PAGES = 8


def _moba_sample_kernel(pt_ref, q_ref, kn_ref, vn_ref, ck_ref, cv_ref, o_ref,
                        buf, sem, s_sc, means_sc, *, layer, n_batch, n_pages, sq):
    b = pl.program_id(0)
    ppb = MOBA_BLOCK // PAGE_SIZE
    nblk = n_pages // ppb
    n_chunks = n_pages // CHUNK_PAGES
    bpc = CHUNK_PAGES // ppb
    nrow = N_HEADS * sq

    def chunk_copies(bb, c, slot):
        src = ck_ref if c < n_chunks else cv_ref
        cc = c % n_chunks
        return [pltpu.make_async_copy(src.at[layer, pt_ref[bb, cc * CHUNK_PAGES + p]],
                                      buf.at[slot, p], sem.at[slot])
                for p in range(CHUNK_PAGES)]

    def start_chunk(bb, c):
        for cp in chunk_copies(bb, c, c % 2):
            cp.start()

    def wait_chunk(bb, c):
        for cp in chunk_copies(bb, c, c % 2):
            cp.wait()

    @pl.when(b == 0)
    def _():
        start_chunk(b, 0)

    qb = q_ref[...]
    qrows = jnp.concatenate([qb] * N_HEADS, axis=0)
    rh = lax.broadcasted_iota(jnp.int32, (nrow, BRANCH_W), 0) >> (sq.bit_length() - 1)
    chh = lax.broadcasted_iota(jnp.int32, (nrow, BRANCH_W), 1) >> (HEAD_DIM.bit_length() - 1)
    head_mask = rh == chh
    qrows = jnp.where(head_mask, qrows, 0.0)
    qs = (qrows * (HEAD_DIM ** -0.5)).astype(BF16)

    for c in range(n_chunks):
        start_chunk(b, c + 1)
        wait_chunk(b, c)
        for n in range(bpc):
            kb = buf[c % 2, n * ppb:(n + 1) * ppb].reshape(MOBA_BLOCK, BRANCH_W)
            blk_id = c * bpc + n
            means_sc[blk_id:blk_id + 1, :] = jnp.sum(kb, axis=0, keepdims=True) * (1.0 / MOBA_BLOCK)
            s_sc[blk_id] = lax.dot_general(qs, kb.astype(BF16), _NT, preferred_element_type=F32)

    gs = _dot_split(qrows, means_sc[...], _NT)
    lane = lax.broadcasted_iota(jnp.int32, gs.shape, 1).astype(F32)
    sel_f = jnp.zeros(gs.shape, F32)
    work = gs
    for _ in range(min(MOBA_TOPK, nblk)):
        mx = jnp.max(work, axis=1, keepdims=True)
        first = jnp.min(jnp.where(work == mx, lane, float(nblk)), axis=1, keepdims=True)
        pick = lane == first
        sel_f = jnp.where(pick, 1.0, sel_f)
        work = jnp.where(pick, -jnp.inf, work)

    s_own = lax.dot_general(qs, kn_ref[...].astype(BF16), _NT, preferred_element_type=F32)
    qi = lax.broadcasted_iota(jnp.int32, (nrow, sq), 0) & (sq - 1)
    kk = lax.broadcasted_iota(jnp.int32, (nrow, sq), 1)
    s_own = jnp.where(kk <= qi, s_own, NEG_INF)
    m = jnp.max(s_own, axis=1, keepdims=True)
    for n in range(nblk):
        sm = jnp.where(sel_f[:, n:n + 1] > 0.5, s_sc[n], NEG_INF)
        m = jnp.maximum(m, jnp.max(sm, axis=1, keepdims=True))
    p_own = jnp.exp(s_own - m)
    l = jnp.sum(p_own, axis=1, keepdims=True)
    acc = jnp.dot(p_own.astype(BF16), vn_ref[...].astype(BF16), preferred_element_type=F32)

    for c in range(n_chunks, 2 * n_chunks):
        if c + 1 < 2 * n_chunks:
            start_chunk(b, c + 1)
        else:
            @pl.when(b + 1 < n_batch)
            def _():
                start_chunk(b + 1, 0)
        wait_chunk(b, c)
        for n in range(bpc):
            blk_id = (c - n_chunks) * bpc + n
            vb = buf[c % 2, n * ppb:(n + 1) * ppb].reshape(MOBA_BLOCK, BRANCH_W)
            p = jnp.exp(jnp.where(sel_f[:, blk_id:blk_id + 1] > 0.5, s_sc[blk_id], NEG_INF) - m)
            l = l + jnp.sum(p, axis=1, keepdims=True)
            acc = acc + jnp.dot(p.astype(BF16), vb.astype(BF16), preferred_element_type=F32)

    full = jnp.where(head_mask, acc / l, 0.0)
    out = full[0:sq]
    for h in range(1, N_HEADS):
        out = out + full[h * sq:(h + 1) * sq]
    o_ref[...] = out.astype(o_ref.dtype)


def _moba_sample(q, k_new, v_new, cache_k, cache_v, page_table, layer, n_batch, sq):
    n_pages = page_table.shape[1]
    assert (n_pages * PAGE_SIZE) % MOBA_BLOCK == 0, "own block must hold only the new rows"
    assert n_pages % CHUNK_PAGES == 0 and (n_pages // CHUNK_PAGES) % 2 == 0
    nblk = n_pages * PAGE_SIZE // MOBA_BLOCK
    depth, n_pool = cache_k.shape[0], cache_k.shape[1]
    ck = cache_k.reshape(depth, n_pool, PAGE_SIZE, BRANCH_W)
    cv = cache_v.reshape(depth, n_pool, PAGE_SIZE, BRANCH_W)
    q3, k3, v3 = (t.reshape(n_batch, sq, BRANCH_W) for t in (q, k_new, v_new))
    row = pl.BlockSpec((None, sq, BRANCH_W), lambda b, pt: (b, 0, 0))
    hbm = pl.BlockSpec(memory_space=pl.ANY)
    out = pl.pallas_call(
        functools.partial(_moba_sample_kernel, layer=layer, n_batch=n_batch, n_pages=n_pages, sq=sq),
        grid_spec=pltpu.PrefetchScalarGridSpec(
            num_scalar_prefetch=1,
            grid=(n_batch,),
            in_specs=[row, row, row, hbm, hbm],
            out_specs=row,
            scratch_shapes=[pltpu.VMEM((2, CHUNK_PAGES, PAGE_SIZE, BRANCH_W), F32),
                            pltpu.SemaphoreType.DMA((2,)),
                            pltpu.VMEM((nblk, N_HEADS * sq, MOBA_BLOCK), F32),
                            pltpu.VMEM((nblk, BRANCH_W), F32)]),
        out_shape=jax.ShapeDtypeStruct((n_batch, sq, BRANCH_W), BF16),
        compiler_params=_cparams(1),
        name="moba_sample",
    )(page_table, q3, k3, v3, ck, cv)
    return out.reshape(n_batch * sq, BRANCH_W)


def _merge_kernel(x_ref, a_ref, b_ref, c_ref, gpre_ref, gpost_ref, wg_ref, proj_ref, wo_ref, o_ref):
    x = x_ref[...]
    u = _rms(x, gpre_ref[...]).astype(BF16)
    gates = jax.nn.sigmoid(jnp.dot(u, wg_ref[...], preferred_element_type=F32))
    merged = None
    for n, br in enumerate((a_ref, b_ref, c_ref)):
        t = gates[:, n * D_MODEL:(n + 1) * D_MODEL] * jnp.dot(br[...], proj_ref[n], preferred_element_type=F32)
        merged = t if merged is None else merged + t
    m = jnp.dot(merged.astype(BF16), wo_ref[...], preferred_element_type=F32)
    o_ref[...] = x + _rms(m, gpost_ref[...])


def _merge(x, a, b, c, gpre, gpost, wg, proj, wo, tm):
    n = x.shape[0]
    row = lambda w: pl.BlockSpec((tm, w), lambda i: (i, 0))
    consts = (gpre, gpost, wg, proj, wo)
    return pl.pallas_call(
        _merge_kernel,
        grid=(n // tm,),
        in_specs=[row(D_MODEL)] + [row(BRANCH_W)] * 3 + [_const_spec(t.shape) for t in consts],
        out_specs=row(D_MODEL),
        out_shape=jax.ShapeDtypeStruct((n, D_MODEL), F32),
        compiler_params=_cparams(1),
        name="merge",
    )(x, a, b, c, *consts)


def _ffn_dense_kernel(x_ref, gpre_ref, gpost_ref, wg_ref, wu_ref, wd_ref, o_ref, u_sc, acc_sc):
    f = pl.program_id(1)

    @pl.when(f == 0)
    def _():
        u_sc[...] = _rms(x_ref[...], gpre_ref[...]).astype(BF16)
        acc_sc[...] = jnp.zeros(acc_sc.shape, F32)

    u = u_sc[...]
    hg = jnp.dot(u, wg_ref[...], preferred_element_type=F32)
    hu = jnp.dot(u, wu_ref[...], preferred_element_type=F32)
    acc_sc[...] += jnp.dot((jax.nn.silu(hg) * hu).astype(BF16), wd_ref[...], preferred_element_type=F32)

    @pl.when(f == pl.num_programs(1) - 1)
    def _():
        o_ref[...] = x_ref[...] + _rms(acc_sc[...], gpost_ref[...])


def _ffn_dense(x, gpre, gpost, wg, wu, wd, tm, fc):
    n = x.shape[0]
    dff = wg.shape[1]
    row = pl.BlockSpec((tm, D_MODEL), lambda i, f: (i, 0))
    return pl.pallas_call(
        _ffn_dense_kernel,
        grid=(n // tm, dff // fc),
        in_specs=[row, _const_spec(gpre.shape), _const_spec(gpost.shape),
                  pl.BlockSpec((D_MODEL, fc), lambda i, f: (0, f)),
                  pl.BlockSpec((D_MODEL, fc), lambda i, f: (0, f)),
                  pl.BlockSpec((fc, D_MODEL), lambda i, f: (f, 0))],
        out_specs=row,
        out_shape=jax.ShapeDtypeStruct((n, D_MODEL), F32),
        scratch_shapes=[pltpu.VMEM((tm, D_MODEL), BF16), pltpu.VMEM((tm, D_MODEL), F32)],
        compiler_params=_cparams(2),
        name="ffn_dense",
    )(x, gpre, gpost, wg, wu, wd)


def _router_gate(u, router_ref):
    logits = _dot_split(u, router_ref[...], _NN)
    lane = lax.broadcasted_iota(jnp.int32, logits.shape, 1)
    logits = jnp.where(lane < N_EXPERTS, logits, -jnp.inf)
    m1 = jnp.max(logits, axis=1, keepdims=True)
    i1 = jnp.min(jnp.where(logits == m1, lane, LANES), axis=1, keepdims=True)
    rest = jnp.where(lane == i1, -jnp.inf, logits)
    m2 = jnp.max(rest, axis=1, keepdims=True)
    i2 = jnp.min(jnp.where(rest == m2, lane, LANES), axis=1, keepdims=True)
    e2 = jnp.exp(m2 - m1)
    w1 = 1.0 / (1.0 + e2)
    w2 = e2 / (1.0 + e2)
    return jnp.where(lane == i1, w1, 0.0) + jnp.where(lane == i2, w2, 0.0)


def _moe_kernel(x_ref, gpre_ref, gpost_ref, router_ref, wg_ref, wu_ref, wd_ref, o_ref,
                u_sc, gate_sc, acc_sc):
    e = pl.program_id(1)
    f = pl.program_id(2)

    @pl.when((e == 0) & (f == 0))
    def _():
        u = _rms(x_ref[...], gpre_ref[...])
        u_sc[...] = u.astype(BF16)
        gate_sc[...] = _router_gate(u, router_ref)
        acc_sc[...] = jnp.zeros(acc_sc.shape, F32)

    u = u_sc[...]
    hg = jnp.dot(u, wg_ref[...], preferred_element_type=F32)
    hu = jnp.dot(u, wu_ref[...], preferred_element_type=F32)
    y = jnp.dot((jax.nn.silu(hg) * hu).astype(BF16), wd_ref[...], preferred_element_type=F32)
    lane = lax.broadcasted_iota(jnp.int32, gate_sc.shape, 1)
    ge = jnp.sum(jnp.where(lane == e, gate_sc[...], 0.0), axis=1, keepdims=True)
    acc_sc[...] += ge * y

    @pl.when((e == pl.num_programs(1) - 1) & (f == pl.num_programs(2) - 1))
    def _():
        o_ref[...] = x_ref[...] + _rms(acc_sc[...], gpost_ref[...])


def _moe(x, gpre, gpost, router_pad, wg, wu, wd, tm, fc):
    n = x.shape[0]
    n_exp, _, dff = wg.shape
    row = pl.BlockSpec((tm, D_MODEL), lambda i, e, f: (i, 0))
    return pl.pallas_call(
        _moe_kernel,
        grid=(n // tm, n_exp, dff // fc),
        in_specs=[row, _const_spec(gpre.shape), _const_spec(gpost.shape), _const_spec(router_pad.shape),
                  pl.BlockSpec((None, D_MODEL, fc), lambda i, e, f: (e, 0, f)),
                  pl.BlockSpec((None, D_MODEL, fc), lambda i, e, f: (e, 0, f)),
                  pl.BlockSpec((None, fc, D_MODEL), lambda i, e, f: (e, f, 0))],
        out_specs=row,
        out_shape=jax.ShapeDtypeStruct((n, D_MODEL), F32),
        scratch_shapes=[pltpu.VMEM((tm, D_MODEL), BF16), pltpu.VMEM((tm, LANES), F32),
                        pltpu.VMEM((tm, D_MODEL), F32)],
        compiler_params=_cparams(3),
        name="moe",
    )(x, gpre, gpost, router_pad, wg, wu, wd)


def _block_diag_halves(w_a, w_i):
    half_blocks = LRU_BLOCKS // 2

    def bd(w4):
        eye = jnp.eye(half_blocks, dtype=w4.dtype)
        return jnp.einsum('hcd,hg->hcgd', w4, eye).reshape(half_blocks * LRU_BLOCK_W, half_blocks * LRU_BLOCK_W)

    halves = [jnp.concatenate([bd(w_a[i * half_blocks:(i + 1) * half_blocks]),
                               bd(w_i[i * half_blocks:(i + 1) * half_blocks])], axis=1) for i in range(2)]
    return jnp.stack(halves).astype(BF16)


def _row(v):
    return v.reshape(1, -1)


def _pad_hist(h, pad_rows):
    return jnp.pad(h, ((0, 0), (pad_rows - h.shape[1], 0), (0, 0)))


PROMPT_TS = 512
ROW_TM = 512
FFN_FC = 1408
MOE_FC = 1792


def kernel(x_prompt, x_sample, cache_k, cache_v, page_table, state_pool, state_conv, state_h, norm_mix_pre, norm_mix_post, norm_ffn_pre, norm_ffn_post, w_in, pool_w, pool_scale, conv_w, conv_b, lru_wa, lru_ba, lru_wi, lru_bi, lru_lambda, proj_pool, proj_attn, proj_lru, w_o, ffn_w_gate, ffn_w_up, ffn_w_down, router_w, moe_w_gate, moe_w_up, moe_w_down):
    depth = w_in.shape[0]
    bp, sp, _ = x_prompt.shape
    bs, ss, _ = x_sample.shape
    past_len = page_table.shape[1] * PAGE_SIZE
    xp = x_prompt.reshape(bp * sp, D_MODEL)
    xs = x_sample.reshape(bs * ss, D_MODEL)
    tm_s = bs * ss

    outs = {name: [] for name in ("kp", "vp", "ks", "vs", "pp", "ps", "cp", "cs", "hp", "hs")}
    for l in range(depth):
        w_qkv = w_in[l, :, :D_QKV].astype(BF16)
        w_gates = w_in[l, :, D_QKV:].astype(BF16)
        mix_consts = (_row(norm_mix_pre[l]), w_qkv, pool_w[l].astype(BF16), _row(pool_scale[l]),
                      conv_w[l], _row(conv_b[l]), _block_diag_halves(lru_wa[l], lru_wi[l]),
                      _row(lru_ba[l]), _row(lru_bi[l]), _row(lru_lambda[l]))
        proj = jnp.stack([proj_pool[l], proj_attn[l], proj_lru[l]]).astype(BF16)
        merge_consts = (_row(norm_mix_pre[l]), _row(norm_mix_post[l]), w_gates, proj, w_o[l].astype(BF16))

        q_p, k_p, v_p, a_p, c_p, pt_p, ct_p, h_p = _mixer_in_prompt(xp, bp, sp, PROMPT_TS, *mix_consts)
        b_p = _moba_prompt(q_p, k_p, v_p, bp, sp)
        xp = _merge(xp, a_p, b_p, c_p, *merge_consts, ROW_TM)

        q_s, k_s, v_s, a_s, c_s, pt_s, ct_s, h_s = _mixer_in_sample(
            xs, bs, ss, past_len, _pad_hist(state_pool[l], POOL_PAD), _pad_hist(state_conv[l], CONV_PAD),
            state_h[l].reshape(bs, 1, BRANCH_W), *mix_consts)
        b_s = _moba_sample(q_s, k_s, v_s, cache_k, cache_v, page_table, l, bs, ss)
        xs = _merge(xs, a_s, b_s, c_s, *merge_consts, tm_s)

        gpre, gpost = _row(norm_ffn_pre[l]), _row(norm_ffn_post[l])
        j = l // 2
        if l % 2 == 0:
            wg, wu, wd = (w[j].astype(BF16) for w in (ffn_w_gate, ffn_w_up, ffn_w_down))
            xp = _ffn_dense(xp, gpre, gpost, wg, wu, wd, ROW_TM, FFN_FC)
            xs = _ffn_dense(xs, gpre, gpost, wg, wu, wd, tm_s, FFN_FC)
        else:
            wg, wu, wd = (w[j].astype(BF16) for w in (moe_w_gate, moe_w_up, moe_w_down))
            router_pad = jnp.pad(router_w[j], ((0, 0), (0, LANES - N_EXPERTS)))
            xp = _moe(xp, gpre, gpost, router_pad, wg, wu, wd, ROW_TM, MOE_FC)
            xs = _moe(xs, gpre, gpost, router_pad, wg, wu, wd, tm_s, MOE_FC)

        outs["kp"].append(k_p.reshape(bp, sp // PAGE_SIZE, PAGE_SIZE, N_HEADS, HEAD_DIM))
        outs["vp"].append(v_p.reshape(bp, sp // PAGE_SIZE, PAGE_SIZE, N_HEADS, HEAD_DIM))
        outs["ks"].append(k_s.reshape(bs, ss, N_HEADS, HEAD_DIM))
        outs["vs"].append(v_s.reshape(bs, ss, N_HEADS, HEAD_DIM))
        outs["pp"].append(pt_p[:, POOL_PAD - POOL_HIST:])
        outs["ps"].append(pt_s[:, POOL_PAD - POOL_HIST:])
        outs["cp"].append(ct_p[:, CONV_PAD - (CONV_W - 1):])
        outs["cs"].append(ct_s[:, CONV_PAD - (CONV_W - 1):])
        outs["hp"].append(h_p.reshape(bp, BRANCH_W))
        outs["hs"].append(h_s.reshape(bs, BRANCH_W))

    st = {k: jnp.stack(v) for k, v in outs.items()}
    return (xp.reshape(bp, sp, D_MODEL), xs.reshape(bs, ss, D_MODEL),
            st["kp"], st["vp"], st["ks"], st["vs"], st["pp"], st["ps"], st["cp"], st["cs"], st["hp"], st["hs"])
```

```python
import functools

import jax
import jax.numpy as jnp
from jax import lax
from jax.experimental import pallas as pl
from jax.experimental.pallas import tpu as pltpu

F32 = jnp.float32
BF16 = jnp.bfloat16

D_MODEL = 1024
BRANCH_W = 512
POOL_WINDOWS = (2, 4, 8, 16)
POOL_GROUP_W = BRANCH_W // len(POOL_WINDOWS)
POOL_HIST = max(POOL_WINDOWS) - 1
POOL_PAD = 16
N_HEADS = 8
HEAD_DIM = BRANCH_W // N_HEADS
MOBA_BLOCK = 256
MOBA_TOPK = 3
PAGE_SIZE = 128
LRU_BLOCKS = 8
LRU_BLOCK_W = BRANCH_W // LRU_BLOCKS
CONV_W = 4
CONV_PAD = 8
LRU_C = 8.0
N_EXPERTS = 8
TOP_K = 2
EPS = 1e-6
NEG_INF = -1e30
LOG2_E = 1.4426950408889634
D_QKV = 6 * BRANCH_W
SUBLANES = 8
LANES = 128
VMEM_LIMIT = 56 * 1024 * 1024


def _cparams(n_axes):
    return pltpu.CompilerParams(
        dimension_semantics=("arbitrary",) * n_axes, vmem_limit_bytes=VMEM_LIMIT)


def _rms(x, g):
    return x * lax.rsqrt(jnp.mean(x * x, axis=-1, keepdims=True) + EPS) * g


def _split_bf16(x):
    hi = x.astype(BF16)
    lo = (x - hi.astype(F32)).astype(BF16)
    return hi, lo


def _dot_split(a, b, dims):
    a_hi, a_lo = _split_bf16(a)
    b_hi, b_lo = _split_bf16(b)
    d = functools.partial(lax.dot_general, dimension_numbers=dims, preferred_element_type=F32)
    return d(a_hi, b_hi) + (d(a_hi, b_lo) + d(a_lo, b_hi))


_NN = (((1,), (0,)), ((), ()))
_NT = (((1,), (1,)), ((), ()))


def _softplus(x):
    return jnp.maximum(x, 0.0) + jnp.log1p(jnp.exp(-jnp.abs(x)))


def _lru_coeffs(xc, wbd_ref, ba, bi, lam, first_row_is_start):
    xcb = xc.astype(BF16)
    half = BRANCH_W // 2
    g0 = jnp.dot(xcb[:, :half], wbd_ref[0], preferred_element_type=F32)
    g1 = jnp.dot(xcb[:, half:], wbd_ref[1], preferred_element_type=F32)
    pre_a = jnp.concatenate([g0[:, :half], g1[:, :half]], axis=1)
    pre_i = jnp.concatenate([g0[:, half:], g1[:, half:]], axis=1)
    r = jax.nn.sigmoid(pre_a + ba)
    i = jax.nn.sigmoid(pre_i + bi)
    log_a = (-LRU_C) * r * _softplus(-lam)
    a = jnp.exp(log_a)
    t = jnp.tanh(log_a)
    mult = jnp.sqrt(-2.0 * t / (1.0 - t))
    if first_row_is_start is not None:
        mult = jnp.where(first_row_is_start, 1.0, mult)
    return a, mult * i * xc


def _scan8(a, u):
    r8 = lax.broadcasted_iota(jnp.int32, a.shape, 0) & (SUBLANES - 1)
    for d in (1, 2, 4):
        a_s = pltpu.roll(a, d, axis=0)
        u_s = pltpu.roll(u, d, axis=0)
        m = r8 >= d
        u = jnp.where(m, u + a * u_s, u)
        a = jnp.where(m, a * a_s, a)
    return a, u


def _pool_mix(win_groups, xp, cnt_groups, poolw_ref, pscale):
    outs = []
    for g in range(len(POOL_WINDOWS)):
        c0 = g * POOL_GROUP_W
        pooled = win_groups[g] / cnt_groups[g] - xp[:, c0:c0 + POOL_GROUP_W]
        outs.append(jnp.dot(pooled.astype(BF16), poolw_ref[g], preferred_element_type=F32))
    return jnp.concatenate(outs, axis=1) * pscale


def _mixer_in_prompt_kernel(x_ref, gpre_ref, win_ref, poolw_ref, pscale_ref, convw_ref, convb_ref,
                            wbd_ref, ba_ref, bi_ref, lam_ref, *rest, ts):
    (qt_ref, kbf_ref, vtb_ref, kmean_ref, kpage_ref, vpage_ref, a_ref, c_ref,
     ptail_ref, ctail_ref, hlast_ref, pbuf, cbuf, hcar) = rest[-14:]
    s = pl.program_id(1)

    @pl.when(s == 0)
    def _():
        pbuf[0:POOL_PAD, :] = jnp.zeros((POOL_PAD, BRANCH_W), F32)
        cbuf[0:CONV_PAD, :] = jnp.zeros((CONV_PAD, BRANCH_W), F32)
        hcar[...] = jnp.zeros((1, BRANCH_W), F32)

    u = _rms(x_ref[...], gpre_ref[...])
    z = jnp.dot(u.astype(BF16), win_ref[...], preferred_element_type=F32)
    bw = BRANCH_W
    xp = z[:, 0:bw]
    k = z[:, 2 * bw:3 * bw]
    xl = z[:, 4 * bw:5 * bw]
    gl = z[:, 5 * bw:6 * bw]

    qt_ref[...] = z[:, bw:2 * bw].T
    kbf_ref[...] = k.astype(BF16)
    kt = k.T
    vt = z[:, 3 * bw:4 * bw].T
    vtb_ref[...] = vt.astype(BF16)
    for n in range(ts // MOBA_BLOCK):
        kmean_ref[n:n + 1, :] = jnp.sum(k[n * MOBA_BLOCK:(n + 1) * MOBA_BLOCK], axis=0, keepdims=True) * (1.0 / MOBA_BLOCK)
    for p in range(ts // PAGE_SIZE):
        kpage_ref[p] = kt[:, p * PAGE_SIZE:(p + 1) * PAGE_SIZE]
        vpage_ref[p] = vt[:, p * PAGE_SIZE:(p + 1) * PAGE_SIZE]

    pos = s * ts + lax.broadcasted_iota(jnp.int32, (ts, 1), 0)

    pbuf[POOL_PAD:POOL_PAD + ts, :] = xp
    wins, cnts = [], []
    for g, w in enumerate(POOL_WINDOWS):
        c0 = g * POOL_GROUP_W
        win = pbuf[POOL_PAD:POOL_PAD + ts, c0:c0 + POOL_GROUP_W]
        for j in range(1, w):
            win = win + pbuf[POOL_PAD - j:POOL_PAD - j + ts, c0:c0 + POOL_GROUP_W]
        wins.append(win)
        cnts.append(jnp.minimum(pos + 1, w).astype(F32))
    a_ref[...] = _pool_mix(wins, xp, cnts, poolw_ref, pscale_ref[...]).astype(a_ref.dtype)
    tail = pbuf[ts:ts + POOL_PAD, :]
    pbuf[0:POOL_PAD, :] = tail
    ptail_ref[...] = tail

    cbuf[CONV_PAD:CONV_PAD + ts, :] = xl
    xc = convb_ref[...] + xl * convw_ref[CONV_W - 1:CONV_W, :]
    for j in range(CONV_W - 1):
        off = CONV_PAD - (CONV_W - 1) + j
        xc = xc + cbuf[off:off + ts, :] * convw_ref[j:j + 1, :]
    ctail = cbuf[ts:ts + CONV_PAD, :]
    cbuf[0:CONV_PAD, :] = ctail
    ctail_ref[...] = ctail

    a, uu = _lru_coeffs(xc, wbd_ref, ba_ref[...], bi_ref[...], lam_ref[...], pos == 0)
    a, uu = _scan8(a, uu)
    carry = jnp.broadcast_to(hcar[...], (SUBLANES, BRANCH_W))
    hs = []
    for g in range(ts // SUBLANES):
        hg = a[g * SUBLANES:(g + 1) * SUBLANES] * carry + uu[g * SUBLANES:(g + 1) * SUBLANES]
        hs.append(hg)
        carry = jnp.broadcast_to(hg[SUBLANES - 1:SUBLANES, :], (SUBLANES, BRANCH_W))
    h = jnp.concatenate(hs, axis=0)
    hcar[...] = carry[0:1, :]
    hlast_ref[...] = carry[0:1, :]
    c_ref[...] = (h * jax.nn.gelu(gl)).astype(c_ref.dtype)


def _const_spec(shape):
    nd = len(shape)
    return pl.BlockSpec(shape, lambda *_: (0,) * nd)


def _mixer_in_prompt(x, n_batch, seq, ts, layer, depth, kv_pages, gpre, win, poolw, pscale, convw, convb,
                     wbd, ba, bi, lam):
    assert ts % MOBA_BLOCK == 0 and seq % ts == 0
    n = n_batch * seq
    nt = seq // ts
    ppt = ts // PAGE_SIZE
    bpt = ts // MOBA_BLOCK
    row = lambda w: pl.BlockSpec((ts, w), lambda b, s: (b * nt + s, 0))
    chan = pl.BlockSpec((None, BRANCH_W, ts), lambda b, s: (b, 0, s))
    per_b = lambda r: pl.BlockSpec((None, r, BRANCH_W), lambda b, s: (b, 0, 0))
    pages = pl.BlockSpec((None, None, ppt, BRANCH_W, PAGE_SIZE), lambda b, s: (layer, b, s, 0, 0))
    page_shape = jax.ShapeDtypeStruct((depth, n_batch, seq // PAGE_SIZE, BRANCH_W, PAGE_SIZE), F32)
    consts = (gpre, win, poolw, pscale, convw, convb, wbd, ba, bi, lam)
    extra = tuple(kv_pages)
    assert all(t.shape == page_shape.shape and t.dtype == page_shape.dtype for t in extra)
    n_in = 1 + len(consts)
    return pl.pallas_call(
        functools.partial(_mixer_in_prompt_kernel, ts=ts),
        grid=(n_batch, nt),
        in_specs=[row(D_MODEL)] + [_const_spec(c.shape) for c in consts]
        + [pl.BlockSpec(memory_space=pl.ANY)] * len(extra),
        out_specs=[chan, row(BRANCH_W), chan,
                   pl.BlockSpec((None, None, bpt, BRANCH_W), lambda b, s: (b, s, 0, 0)),
                   pages, pages, row(BRANCH_W), row(BRANCH_W),
                   per_b(POOL_PAD), per_b(CONV_PAD), per_b(1)],
        out_shape=[jax.ShapeDtypeStruct((n_batch, BRANCH_W, seq), F32),
                   jax.ShapeDtypeStruct((n, BRANCH_W), BF16),
                   jax.ShapeDtypeStruct((n_batch, BRANCH_W, seq), BF16),
                   jax.ShapeDtypeStruct((n_batch, nt, bpt, BRANCH_W), F32),
                   page_shape, page_shape,
                   jax.ShapeDtypeStruct((n, BRANCH_W), BF16),
                   jax.ShapeDtypeStruct((n, BRANCH_W), BF16),
                   jax.ShapeDtypeStruct((n_batch, POOL_PAD, BRANCH_W), F32),
                   jax.ShapeDtypeStruct((n_batch, CONV_PAD, BRANCH_W), F32),
                   jax.ShapeDtypeStruct((n_batch, 1, BRANCH_W), F32)],
        scratch_shapes=[pltpu.VMEM((POOL_PAD + ts, BRANCH_W), F32),
                        pltpu.VMEM((CONV_PAD + ts, BRANCH_W), F32),
                        pltpu.VMEM((1, BRANCH_W), F32)],
        input_output_aliases={n_in + i: 4 + i for i in range(len(extra))},
        compiler_params=_cparams(2),
        name="mixer_in_prompt",
    )(x, *consts, *extra)


def _mixer_in_sample_kernel(x_ref, phist_ref, chist_ref, h0_ref, gpre_ref, win_ref, poolw_ref,
                            pscale_ref, convw_ref, convb_ref, wbd_ref, ba_ref, bi_ref, lam_ref,
                            q_ref, k_ref, v_ref, a_ref, c_ref, ptail_ref, ctail_ref, hlast_ref,
                            pbuf, cbuf, *, n_batch, sq, start):
    rows = n_batch * sq
    u = _rms(x_ref[...], gpre_ref[...])
    z = jnp.dot(u.astype(BF16), win_ref[...], preferred_element_type=F32)
    bw = BRANCH_W
    xp = z[:, 0:bw]
    q_ref[...] = z[:, bw:2 * bw]
    k_ref[...] = z[:, 2 * bw:3 * bw]
    v_ref[...] = z[:, 3 * bw:4 * bw]
    xl = z[:, 4 * bw:5 * bw]
    gl = z[:, 5 * bw:6 * bw]

    pos = start + (lax.broadcasted_iota(jnp.int32, (rows, 1), 0) & (sq - 1))

    pbuf[:, 0:POOL_PAD, :] = phist_ref[...]
    pbuf[:, POOL_PAD:POOL_PAD + sq, :] = xp.reshape(n_batch, sq, bw)
    wins, cnts = [], []
    for g, w in enumerate(POOL_WINDOWS):
        c0 = g * POOL_GROUP_W
        win = pbuf[:, POOL_PAD:POOL_PAD + sq, c0:c0 + POOL_GROUP_W]
        for j in range(1, w):
            win = win + pbuf[:, POOL_PAD - j:POOL_PAD - j + sq, c0:c0 + POOL_GROUP_W]
        wins.append(win.reshape(rows, POOL_GROUP_W))
        cnts.append(jnp.minimum(pos + 1, w).astype(F32))
    a_ref[...] = _pool_mix(wins, xp, cnts, poolw_ref, pscale_ref[...]).astype(a_ref.dtype)
    ptail_ref[...] = pbuf[:, sq:sq + POOL_PAD, :]

    cbuf[:, 0:CONV_PAD, :] = chist_ref[...]
    cbuf[:, CONV_PAD:CONV_PAD + sq, :] = xl.reshape(n_batch, sq, bw)
    xc = xl.reshape(n_batch, sq, bw) * convw_ref[CONV_W - 1:CONV_W, :]
    for j in range(CONV_W - 1):
        off = CONV_PAD - (CONV_W - 1) + j
        xc = xc + cbuf[:, off:off + sq, :] * convw_ref[j:j + 1, :]
    xc = xc.reshape(rows, bw) + convb_ref[...]
    ctail_ref[...] = cbuf[:, sq:sq + CONV_PAD, :]

    a, uu = _lru_coeffs(xc, wbd_ref, ba_ref[...], bi_ref[...], lam_ref[...],
                        (pos == 0) if start == 0 else None)
    a, uu = _scan8(a, uu)
    h = a.reshape(n_batch, sq, bw) * h0_ref[...] + uu.reshape(n_batch, sq, bw)
    hlast_ref[...] = h[:, sq - 1:sq, :]
    c_ref[...] = (h.reshape(rows, bw) * jax.nn.gelu(gl)).astype(c_ref.dtype)


def _mixer_in_sample(x, n_batch, sq, start, phist, chist, h0, gpre, win, poolw, pscale, convw, convb,
                     wbd, ba, bi, lam):
    assert sq == SUBLANES, "sample sequences must fill exactly one sublane group"
    rows = n_batch * sq
    args = (x, phist, chist, h0, gpre, win, poolw, pscale, convw, convb, wbd, ba, bi, lam)
    return pl.pallas_call(
        functools.partial(_mixer_in_sample_kernel, n_batch=n_batch, sq=sq, start=start),
        grid=(1,),
        in_specs=[_const_spec(a.shape) for a in args],
        out_specs=[_const_spec((rows, BRANCH_W))] * 5
        + [_const_spec((n_batch, POOL_PAD, BRANCH_W)), _const_spec((n_batch, CONV_PAD, BRANCH_W)),
           _const_spec((n_batch, 1, BRANCH_W))],
        out_shape=[jax.ShapeDtypeStruct((rows, BRANCH_W), F32)] * 3
        + [jax.ShapeDtypeStruct((rows, BRANCH_W), BF16)] * 2
        + [jax.ShapeDtypeStruct((n_batch, POOL_PAD, BRANCH_W), F32),
           jax.ShapeDtypeStruct((n_batch, CONV_PAD, BRANCH_W), F32),
           jax.ShapeDtypeStruct((n_batch, 1, BRANCH_W), F32)],
        scratch_shapes=[pltpu.VMEM((n_batch, POOL_PAD + sq, BRANCH_W), F32),
                        pltpu.VMEM((n_batch, CONV_PAD + sq, BRANCH_W), F32)],
        compiler_params=_cparams(1),
        name="mixer_in_sample",
    )(*args)


def _topk_rows(g, n_valid, k):
    ridx = lax.broadcasted_iota(jnp.int32, g.shape, 0)
    valid = ridx < n_valid
    rows = []
    for j in range(n_valid):
        gj = g[j:j + 1, :]
        beats = jnp.where(ridx < j, jnp.where(g >= gj, 1.0, 0.0), jnp.where(g > gj, 1.0, 0.0))
        cnt = jnp.sum(jnp.where(valid, beats, 0.0), axis=0, keepdims=True)
        rows.append(cnt < float(k))
    return rows


def _moba_prompt_block(i, qt_ref, k_ref, vt_ref, means_ref, o_ref):
    blk = MOBA_BLOCK
    pair_w = 2 * HEAD_DIM
    nk = (i + 1) * blk
    qt = qt_ref[...]
    ch = lax.broadcasted_iota(jnp.int32, (pair_w, 1), 0)
    key_r = lax.broadcasted_iota(jnp.int32, (blk, blk), 0)
    qry_c = lax.broadcasted_iota(jnp.int32, (blk, blk), 1)
    causal = key_r <= qry_c
    scale = HEAD_DIM ** -0.5 * LOG2_E

    outs = []
    for h in range(2):
        qh = jnp.where((ch >= h * HEAD_DIM) & (ch < (h + 1) * HEAD_DIM), qt, 0.0)
        qs = (qh * scale).astype(BF16)
        st = jnp.dot(k_ref[0:nk, :], qs, preferred_element_type=F32)
        sel = _topk_rows(_dot_split(means_ref[...], qh, _NN), i, MOBA_TOPK) if i > MOBA_TOPK else None
        pieces = []
        for j in range(i):
            sj = st[j * blk:(j + 1) * blk]
            pieces.append(sj if sel is None else jnp.where(sel[j], sj, NEG_INF))
        pieces.append(jnp.where(causal, st[i * blk:], NEG_INF))
        sm = jnp.concatenate(pieces, axis=0) if i else pieces[0]
        m = jnp.max(sm, axis=0, keepdims=True)
        p = jnp.exp2(sm - m)
        l = jnp.sum(p, axis=0, keepdims=True)
        acc = jnp.dot(vt_ref[:, 0:nk], p.astype(BF16), preferred_element_type=F32)
        outs.append(acc / l)

    out_t = jnp.where(ch < HEAD_DIM, outs[0], outs[1])
    o_ref[...] = out_t.T.astype(o_ref.dtype)


def _moba_prompt_kernel(qt_ref, k_ref, vt_ref, means_ref, o_ref, *, nblk):
    i = pl.program_id(2)
    for c in range(nblk):
        pl.when(i == c)(functools.partial(_moba_prompt_block, c, qt_ref, k_ref, vt_ref, means_ref, o_ref))


def _moba_prompt(qt, kbf, vtb, means, n_batch, seq):
    assert seq % MOBA_BLOCK == 0
    nblk = seq // MOBA_BLOCK
    pair_w = 2 * HEAD_DIM
    n_pairs = BRANCH_W // pair_w
    out = pl.pallas_call(
        functools.partial(_moba_prompt_kernel, nblk=nblk),
        grid=(n_batch, n_pairs, nblk),
        in_specs=[pl.BlockSpec((None, pair_w, MOBA_BLOCK), lambda b, hp, i: (b, hp, i)),
                  pl.BlockSpec((None, seq, pair_w), lambda b, hp, i: (b, 0, hp)),
                  pl.BlockSpec((None, pair_w, seq), lambda b, hp, i: (b, hp, 0)),
                  pl.BlockSpec((None, nblk, pair_w), lambda b, hp, i: (b, 0, hp))],
        out_specs=pl.BlockSpec((None, MOBA_BLOCK, pair_w), lambda b, hp, i: (b, i, hp)),
        out_shape=jax.ShapeDtypeStruct((n_batch, seq, BRANCH_W), BF16),
        compiler_params=_cparams(3),
        name="moba_prompt",
    )(qt, kbf.reshape(n_batch, seq, BRANCH_W), vtb, means)
    return out.reshape(n_batch * seq, BRANCH_W)


CHUNK_PAGES = 8


def _moba_sample_kernel(pt_ref, q_ref, kn_ref, vn_ref, ck_ref, cv_ref, o_ref,
                        buf, sem, s_sc, means_sc, *, layer, n_batch, n_pages, sq):
    b = pl.program_id(0)
    ppb = MOBA_BLOCK // PAGE_SIZE
    nblk = n_pages // ppb
    n_chunks = n_pages // CHUNK_PAGES
    bpc = CHUNK_PAGES // ppb
    nrow = N_HEADS * sq

    def chunk_copies(bb, c, slot):
        src = ck_ref if c < n_chunks else cv_ref
        cc = c % n_chunks
        return [pltpu.make_async_copy(src.at[layer, pt_ref[bb, cc * CHUNK_PAGES + p]],
                                      buf.at[slot, p], sem.at[slot])
                for p in range(CHUNK_PAGES)]

    def start_chunk(bb, c):
        for cp in chunk_copies(bb, c, c % 2):
            cp.start()

    def wait_chunk(bb, c):
        for cp in chunk_copies(bb, c, c % 2):
            cp.wait()

    @pl.when(b == 0)
    def _():
        start_chunk(b, 0)

    qb = q_ref[...]
    qrows = jnp.concatenate([qb] * N_HEADS, axis=0)
    rh = lax.broadcasted_iota(jnp.int32, (nrow, BRANCH_W), 0) >> (sq.bit_length() - 1)
    chh = lax.broadcasted_iota(jnp.int32, (nrow, BRANCH_W), 1) >> (HEAD_DIM.bit_length() - 1)
    head_mask = rh == chh
    qrows = jnp.where(head_mask, qrows, 0.0)
    qs = (qrows * (HEAD_DIM ** -0.5)).astype(BF16)

    mlane = lax.broadcasted_iota(jnp.int32, (BRANCH_W, LANES), 1)
    means_sc[...] = jnp.zeros((BRANCH_W, LANES), F32)
    for c in range(n_chunks):
        start_chunk(b, c + 1)
        wait_chunk(b, c)
        for n in range(bpc):
            blk_id = c * bpc + n
            ksum = None
            for pg in range(n * ppb, (n + 1) * ppb):
                kt = buf[c % 2, pg]
                ksum = kt if ksum is None else ksum + kt
                s_sc[c * CHUNK_PAGES + pg] = jnp.dot(qs, kt.astype(BF16), preferred_element_type=F32)
            kmean = jnp.sum(ksum, axis=1, keepdims=True) * (1.0 / MOBA_BLOCK)
            means_sc[...] = jnp.where(mlane == blk_id, kmean, means_sc[...])

    gs = _dot_split(qrows, means_sc[...], _NN)
    lane_i = lax.broadcasted_iota(jnp.int32, gs.shape, 1)
    lane = lane_i.astype(F32)
    sel_f = jnp.zeros(gs.shape, F32)
    work = jnp.where(lane_i < nblk, gs, -jnp.inf)
    for _ in range(min(MOBA_TOPK, nblk)):
        mx = jnp.max(work, axis=1, keepdims=True)
        first = jnp.min(jnp.where(work == mx, lane, float(nblk)), axis=1, keepdims=True)
        pick = lane == first
        sel_f = jnp.where(pick, 1.0, sel_f)
        work = jnp.where(pick, -jnp.inf, work)

    s_own = lax.dot_general(qs, kn_ref[...].astype(BF16), _NT, preferred_element_type=F32)
    qi = lax.broadcasted_iota(jnp.int32, (nrow, sq), 0) & (sq - 1)
    kk = lax.broadcasted_iota(jnp.int32, (nrow, sq), 1)
    s_own = jnp.where(kk <= qi, s_own, NEG_INF)
    m = jnp.max(s_own, axis=1, keepdims=True)
    for pg in range(n_pages):
        sm = jnp.where(sel_f[:, pg // ppb:pg // ppb + 1] > 0.5, s_sc[pg], NEG_INF)
        m = jnp.maximum(m, jnp.max(sm, axis=1, keepdims=True))
    p_own = jnp.exp(s_own - m)
    l = jnp.sum(p_own, axis=1, keepdims=True)
    acc = jnp.dot(p_own.astype(BF16), vn_ref[...].astype(BF16), preferred_element_type=F32)

    for c in range(n_chunks, 2 * n_chunks):
        if c + 1 < 2 * n_chunks:
            start_chunk(b, c + 1)
        else:
            @pl.when(b + 1 < n_batch)
            def _():
                start_chunk(b + 1, 0)
        wait_chunk(b, c)
        for pg in range(CHUNK_PAGES):
            page = (c - n_chunks) * CHUNK_PAGES + pg
            blk_id = page // ppb
            vt = buf[c % 2, pg]
            p = jnp.exp(jnp.where(sel_f[:, blk_id:blk_id + 1] > 0.5, s_sc[page], NEG_INF) - m)
            l = l + jnp.sum(p, axis=1, keepdims=True)
            acc = acc + lax.dot_general(p.astype(BF16), vt.astype(BF16), _NT, preferred_element_type=F32)

    full = jnp.where(head_mask, acc / l, 0.0)
    out = full[0:sq]
    for h in range(1, N_HEADS):
        out = out + full[h * sq:(h + 1) * sq]
    o_ref[...] = out.astype(o_ref.dtype)


def _moba_sample(q, k_new, v_new, cache_k, cache_v, page_table, layer, n_batch, sq):
    n_pages = page_table.shape[1]
    assert (n_pages * PAGE_SIZE) % MOBA_BLOCK == 0, "own block must hold only the new rows"
    assert n_pages % CHUNK_PAGES == 0 and (n_pages // CHUNK_PAGES) % 2 == 0
    nblk = n_pages * PAGE_SIZE // MOBA_BLOCK
    depth, n_pool = cache_k.shape[0], cache_k.shape[1]
    ck = cache_k.transpose(0, 1, 3, 4, 2).reshape(depth, n_pool, BRANCH_W, PAGE_SIZE)
    cv = cache_v.transpose(0, 1, 3, 4, 2).reshape(depth, n_pool, BRANCH_W, PAGE_SIZE)
    q3, k3, v3 = (t.reshape(n_batch, sq, BRANCH_W) for t in (q, k_new, v_new))
    row = pl.BlockSpec((None, sq, BRANCH_W), lambda b, pt: (b, 0, 0))
    hbm = pl.BlockSpec(memory_space=pl.ANY)
    out = pl.pallas_call(
        functools.partial(_moba_sample_kernel, layer=layer, n_batch=n_batch, n_pages=n_pages, sq=sq),
        grid_spec=pltpu.PrefetchScalarGridSpec(
            num_scalar_prefetch=1,
            grid=(n_batch,),
            in_specs=[row, row, row, hbm, hbm],
            out_specs=row,
            scratch_shapes=[pltpu.VMEM((2, CHUNK_PAGES, BRANCH_W, PAGE_SIZE), F32),
                            pltpu.SemaphoreType.DMA((2,)),
                            pltpu.VMEM((n_pages, N_HEADS * sq, PAGE_SIZE), F32),
                            pltpu.VMEM((BRANCH_W, LANES), F32)]),
        out_shape=jax.ShapeDtypeStruct((n_batch, sq, BRANCH_W), BF16),
        compiler_params=_cparams(1),
        name="moba_sample",
    )(page_table, q3, k3, v3, ck, cv)
    return out.reshape(n_batch * sq, BRANCH_W)


def _merge_kernel(x_ref, a_ref, b_ref, c_ref, gpre_ref, gpost_ref, wg_ref, proj_ref, wo_ref, o_ref):
    x = x_ref[...]
    u = _rms(x, gpre_ref[...]).astype(BF16)
    gates = jax.nn.sigmoid(jnp.dot(u, wg_ref[...], preferred_element_type=F32))
    merged = None
    for n, br in enumerate((a_ref, b_ref, c_ref)):
        t = gates[:, n * D_MODEL:(n + 1) * D_MODEL] * jnp.dot(br[...], proj_ref[n], preferred_element_type=F32)
        merged = t if merged is None else merged + t
    m = jnp.dot(merged.astype(BF16), wo_ref[...], preferred_element_type=F32)
    o_ref[...] = x + _rms(m, gpost_ref[...])


def _merge(x, a, b, c, gpre, gpost, wg, proj, wo, tm):
    n = x.shape[0]
    row = lambda w: pl.BlockSpec((tm, w), lambda i: (i, 0))
    consts = (gpre, gpost, wg, proj, wo)
    return pl.pallas_call(
        _merge_kernel,
        grid=(n // tm,),
        in_specs=[row(D_MODEL)] + [row(BRANCH_W)] * 3 + [_const_spec(t.shape) for t in consts],
        out_specs=row(D_MODEL),
        out_shape=jax.ShapeDtypeStruct((n, D_MODEL), F32),
        compiler_params=_cparams(1),
        name="merge",
    )(x, a, b, c, *consts)


def _ffn_dense_kernel(x_ref, gpre_ref, gpost_ref, wg_ref, wu_ref, wd_ref, o_ref, u_sc, acc_sc):
    f = pl.program_id(1)

    @pl.when(f == 0)
    def _():
        u_sc[...] = _rms(x_ref[...], gpre_ref[...]).astype(BF16)
        acc_sc[...] = jnp.zeros(acc_sc.shape, F32)

    u = u_sc[...]
    hg = jnp.dot(u, wg_ref[...], preferred_element_type=F32)
    hu = jnp.dot(u, wu_ref[...], preferred_element_type=F32)
    acc_sc[...] += jnp.dot((jax.nn.silu(hg) * hu).astype(BF16), wd_ref[...], preferred_element_type=F32)

    @pl.when(f == pl.num_programs(1) - 1)
    def _():
        o_ref[...] = x_ref[...] + _rms(acc_sc[...], gpost_ref[...])


def _ffn_dense(x, gpre, gpost, wg, wu, wd, tm, fc):
    n = x.shape[0]
    dff = wg.shape[1]
    row = pl.BlockSpec((tm, D_MODEL), lambda i, f: (i, 0))
    return pl.pallas_call(
        _ffn_dense_kernel,
        grid=(n // tm, dff // fc),
        in_specs=[row, _const_spec(gpre.shape), _const_spec(gpost.shape),
                  pl.BlockSpec((D_MODEL, fc), lambda i, f: (0, f)),
                  pl.BlockSpec((D_MODEL, fc), lambda i, f: (0, f)),
                  pl.BlockSpec((fc, D_MODEL), lambda i, f: (f, 0))],
        out_specs=row,
        out_shape=jax.ShapeDtypeStruct((n, D_MODEL), F32),
        scratch_shapes=[pltpu.VMEM((tm, D_MODEL), BF16), pltpu.VMEM((tm, D_MODEL), F32)],
        compiler_params=_cparams(2),
        name="ffn_dense",
    )(x, gpre, gpost, wg, wu, wd)


def _router_gate(u, router_ref):
    logits = _dot_split(u, router_ref[...], _NN)
    lane = lax.broadcasted_iota(jnp.int32, logits.shape, 1)
    logits = jnp.where(lane < N_EXPERTS, logits, -jnp.inf)
    m1 = jnp.max(logits, axis=1, keepdims=True)
    i1 = jnp.min(jnp.where(logits == m1, lane, LANES), axis=1, keepdims=True)
    rest = jnp.where(lane == i1, -jnp.inf, logits)
    m2 = jnp.max(rest, axis=1, keepdims=True)
    i2 = jnp.min(jnp.where(rest == m2, lane, LANES), axis=1, keepdims=True)
    e2 = jnp.exp(m2 - m1)
    w1 = 1.0 / (1.0 + e2)
    w2 = e2 / (1.0 + e2)
    return jnp.where(lane == i1, w1, 0.0) + jnp.where(lane == i2, w2, 0.0)


def _moe_kernel(x_ref, gpre_ref, gpost_ref, router_ref, wg_ref, wu_ref, wd_ref, o_ref,
                u_sc, gate_sc, acc_sc):
    e = pl.program_id(1)
    f = pl.program_id(2)

    @pl.when((e == 0) & (f == 0))
    def _():
        u = _rms(x_ref[...], gpre_ref[...])
        u_sc[...] = u.astype(BF16)
        gate_sc[...] = _router_gate(u, router_ref)
        acc_sc[...] = jnp.zeros(acc_sc.shape, F32)

    u = u_sc[...]
    hg = jnp.dot(u, wg_ref[...], preferred_element_type=F32)
    hu = jnp.dot(u, wu_ref[...], preferred_element_type=F32)
    y = jnp.dot((jax.nn.silu(hg) * hu).astype(BF16), wd_ref[...], preferred_element_type=F32)
    lane = lax.broadcasted_iota(jnp.int32, gate_sc.shape, 1)
    ge = jnp.sum(jnp.where(lane == e, gate_sc[...], 0.0), axis=1, keepdims=True)
    acc_sc[...] += ge * y

    @pl.when((e == pl.num_programs(1) - 1) & (f == pl.num_programs(2) - 1))
    def _():
        o_ref[...] = x_ref[...] + _rms(acc_sc[...], gpost_ref[...])


def _moe(x, gpre, gpost, router_pad, wg, wu, wd, tm, fc):
    n = x.shape[0]
    n_exp, _, dff = wg.shape
    row = pl.BlockSpec((tm, D_MODEL), lambda i, e, f: (i, 0))
    return pl.pallas_call(
        _moe_kernel,
        grid=(n // tm, n_exp, dff // fc),
        in_specs=[row, _const_spec(gpre.shape), _const_spec(gpost.shape), _const_spec(router_pad.shape),
                  pl.BlockSpec((None, D_MODEL, fc), lambda i, e, f: (e, 0, f)),
                  pl.BlockSpec((None, D_MODEL, fc), lambda i, e, f: (e, 0, f)),
                  pl.BlockSpec((None, fc, D_MODEL), lambda i, e, f: (e, f, 0))],
        out_specs=row,
        out_shape=jax.ShapeDtypeStruct((n, D_MODEL), F32),
        scratch_shapes=[pltpu.VMEM((tm, D_MODEL), BF16), pltpu.VMEM((tm, LANES), F32),
                        pltpu.VMEM((tm, D_MODEL), F32)],
        compiler_params=_cparams(3),
        name="moe",
    )(x, gpre, gpost, router_pad, wg, wu, wd)


def _block_diag_halves(w_a, w_i):
    half_blocks = LRU_BLOCKS // 2

    def bd(w4):
        eye = jnp.eye(half_blocks, dtype=w4.dtype)
        return jnp.einsum('hcd,hg->hcgd', w4, eye).reshape(half_blocks * LRU_BLOCK_W, half_blocks * LRU_BLOCK_W)

    halves = [jnp.concatenate([bd(w_a[i * half_blocks:(i + 1) * half_blocks]),
                               bd(w_i[i * half_blocks:(i + 1) * half_blocks])], axis=1) for i in range(2)]
    return jnp.stack(halves).astype(BF16)


def _row(v):
    return v.reshape(1, -1)


def _pad_hist(h, pad_rows):
    return jnp.pad(h, ((0, 0), (pad_rows - h.shape[1], 0), (0, 0)))


PROMPT_TS = 512
ROW_TM = 512
FFN_FC = 1408
MOE_FC = 1792


def kernel(x_prompt, x_sample, cache_k, cache_v, page_table, state_pool, state_conv, state_h, norm_mix_pre, norm_mix_post, norm_ffn_pre, norm_ffn_post, w_in, pool_w, pool_scale, conv_w, conv_b, lru_wa, lru_ba, lru_wi, lru_bi, lru_lambda, proj_pool, proj_attn, proj_lru, w_o, ffn_w_gate, ffn_w_up, ffn_w_down, router_w, moe_w_gate, moe_w_up, moe_w_down):
    depth = w_in.shape[0]
    bp, sp, _ = x_prompt.shape
    bs, ss, _ = x_sample.shape
    past_len = page_table.shape[1] * PAGE_SIZE
    xp = x_prompt.reshape(bp * sp, D_MODEL)
    xs = x_sample.reshape(bs * ss, D_MODEL)
    tm_s = bs * ss

    outs = {name: [] for name in ("ks", "vs", "pp", "ps", "cp", "cs", "hp", "hs")}
    kv_pages = tuple(jnp.zeros((depth, bp, sp // PAGE_SIZE, BRANCH_W, PAGE_SIZE), F32) for _ in range(2))
    for l in range(depth):
        w_qkv = w_in[l, :, :D_QKV].astype(BF16)
        w_gates = w_in[l, :, D_QKV:].astype(BF16)
        mix_consts = (_row(norm_mix_pre[l]), w_qkv, pool_w[l].astype(BF16), _row(pool_scale[l]),
                      conv_w[l], _row(conv_b[l]), _block_diag_halves(lru_wa[l], lru_wi[l]),
                      _row(lru_ba[l]), _row(lru_bi[l]), _row(lru_lambda[l]))
        proj = jnp.stack([proj_pool[l], proj_attn[l], proj_lru[l]]).astype(BF16)
        merge_consts = (_row(norm_mix_pre[l]), _row(norm_mix_post[l]), w_gates, proj, w_o[l].astype(BF16))

        qt_p, kbf_p, vtb_p, kmean_p, kpages, vpages, a_p, c_p, pt_p, ct_p, h_p = _mixer_in_prompt(
            xp, bp, sp, PROMPT_TS, l, depth, kv_pages, *mix_consts)
        kv_pages = (kpages, vpages)
        b_p = _moba_prompt(qt_p, kbf_p, vtb_p, kmean_p.reshape(bp, sp // MOBA_BLOCK, BRANCH_W), bp, sp)
        xp = _merge(xp, a_p, b_p, c_p, *merge_consts, ROW_TM)

        q_s, k_s, v_s, a_s, c_s, pt_s, ct_s, h_s = _mixer_in_sample(
            xs, bs, ss, past_len, _pad_hist(state_pool[l], POOL_PAD), _pad_hist(state_conv[l], CONV_PAD),
            state_h[l].reshape(bs, 1, BRANCH_W), *mix_consts)
        b_s = _moba_sample(q_s, k_s, v_s, cache_k, cache_v, page_table, l, bs, ss)
        xs = _merge(xs, a_s, b_s, c_s, *merge_consts, tm_s)

        gpre, gpost = _row(norm_ffn_pre[l]), _row(norm_ffn_post[l])
        j = l // 2
        if l % 2 == 0:
            wg, wu, wd = (w[j].astype(BF16) for w in (ffn_w_gate, ffn_w_up, ffn_w_down))
            xp = _ffn_dense(xp, gpre, gpost, wg, wu, wd, ROW_TM, FFN_FC)
            xs = _ffn_dense(xs, gpre, gpost, wg, wu, wd, tm_s, FFN_FC)
        else:
            wg, wu, wd = (w[j].astype(BF16) for w in (moe_w_gate, moe_w_up, moe_w_down))
            router_pad = jnp.pad(router_w[j], ((0, 0), (0, LANES - N_EXPERTS)))
            xp = _moe(xp, gpre, gpost, router_pad, wg, wu, wd, ROW_TM, MOE_FC)
            xs = _moe(xs, gpre, gpost, router_pad, wg, wu, wd, tm_s, MOE_FC)

        outs["ks"].append(k_s.reshape(bs, ss, N_HEADS, HEAD_DIM))
        outs["vs"].append(v_s.reshape(bs, ss, N_HEADS, HEAD_DIM))
        outs["pp"].append(pt_p[:, POOL_PAD - POOL_HIST:])
        outs["ps"].append(pt_s[:, POOL_PAD - POOL_HIST:])
        outs["cp"].append(ct_p[:, CONV_PAD - (CONV_W - 1):])
        outs["cs"].append(ct_s[:, CONV_PAD - (CONV_W - 1):])
        outs["hp"].append(h_p.reshape(bp, BRANCH_W))
        outs["hs"].append(h_s.reshape(bs, BRANCH_W))

    st = {k: jnp.stack(v) for k, v in outs.items()}
    kp, vp = (t.reshape(depth, bp, sp // PAGE_SIZE, N_HEADS, HEAD_DIM, PAGE_SIZE).transpose(0, 1, 2, 5, 3, 4)
              for t in kv_pages)
    return (xp.reshape(bp, sp, D_MODEL), xs.reshape(bs, ss, D_MODEL),
            kp, vp, st["ks"], st["vs"], st["pp"], st["ps"], st["cp"], st["cs"], st["hp"], st["hs"])
```

```python
import functools

import jax
import jax.numpy as jnp
from jax import lax
from jax.experimental import pallas as pl
from jax.experimental.pallas import tpu as pltpu

F32 = jnp.float32
BF16 = jnp.bfloat16

D_MODEL = 1024
BRANCH_W = 512
POOL_WINDOWS = (2, 4, 8, 16)
POOL_GROUP_W = BRANCH_W // len(POOL_WINDOWS)
POOL_HIST = max(POOL_WINDOWS) - 1
POOL_PAD = 16
N_HEADS = 8
HEAD_DIM = BRANCH_W // N_HEADS
MOBA_BLOCK = 256
MOBA_TOPK = 3
PAGE_SIZE = 128
LRU_BLOCKS = 8
LRU_BLOCK_W = BRANCH_W // LRU_BLOCKS
CONV_W = 4
CONV_PAD = 8
LRU_C = 8.0
N_EXPERTS = 8
TOP_K = 2
EPS = 1e-6
NEG_INF = -1e30
LOG2_E = 1.4426950408889634
D_QKV = 6 * BRANCH_W
SUBLANES = 8
LANES = 128
VMEM_LIMIT = 56 * 1024 * 1024


def _cparams(n_axes):
    return pltpu.CompilerParams(
        dimension_semantics=("arbitrary",) * n_axes, vmem_limit_bytes=VMEM_LIMIT)


def _rms(x, g):
    return x * lax.rsqrt(jnp.mean(x * x, axis=-1, keepdims=True) + EPS) * g


def _split_bf16(x):
    hi = x.astype(BF16)
    lo = (x - hi.astype(F32)).astype(BF16)
    return hi, lo


def _dot_split(a, b, dims):
    a_hi, a_lo = _split_bf16(a)
    b_hi, b_lo = _split_bf16(b)
    d = functools.partial(lax.dot_general, dimension_numbers=dims, preferred_element_type=F32)
    return d(a_hi, b_hi) + (d(a_hi, b_lo) + d(a_lo, b_hi))


_NN = (((1,), (0,)), ((), ()))
_NT = (((1,), (1,)), ((), ()))


def _softplus(x):
    return jnp.maximum(x, 0.0) + jnp.log1p(jnp.exp(-jnp.abs(x)))


def _lru_coeffs(xc, wbd_ref, ba, bi, lam, first_row_is_start):
    xcb = xc.astype(BF16)
    half = BRANCH_W // 2
    g0 = jnp.dot(xcb[:, :half], wbd_ref[0], preferred_element_type=F32)
    g1 = jnp.dot(xcb[:, half:], wbd_ref[1], preferred_element_type=F32)
    pre_a = jnp.concatenate([g0[:, :half], g1[:, :half]], axis=1)
    pre_i = jnp.concatenate([g0[:, half:], g1[:, half:]], axis=1)
    r = jax.nn.sigmoid(pre_a + ba)
    i = jax.nn.sigmoid(pre_i + bi)
    log_a = (-LRU_C) * r * _softplus(-lam)
    a = jnp.exp(log_a)
    t = jnp.tanh(log_a)
    mult = jnp.sqrt(-2.0 * t / (1.0 - t))
    if first_row_is_start is not None:
        mult = jnp.where(first_row_is_start, 1.0, mult)
    return a, mult * i * xc


def _scan8(a, u):
    r8 = lax.broadcasted_iota(jnp.int32, a.shape, 0) & (SUBLANES - 1)
    for d in (1, 2, 4):
        a_s = pltpu.roll(a, d, axis=0)
        u_s = pltpu.roll(u, d, axis=0)
        m = r8 >= d
        u = jnp.where(m, u + a * u_s, u)
        a = jnp.where(m, a * a_s, a)
    return a, u


def _pool_mix(win_groups, xp, cnt_groups, poolw_ref, pscale):
    outs = []
    for g in range(len(POOL_WINDOWS)):
        c0 = g * POOL_GROUP_W
        pooled = win_groups[g] / cnt_groups[g] - xp[:, c0:c0 + POOL_GROUP_W]
        outs.append(jnp.dot(pooled.astype(BF16), poolw_ref[g], preferred_element_type=F32))
    return jnp.concatenate(outs, axis=1) * pscale


def _mixer_in_prompt_kernel(x_ref, gpre_ref, win_ref, poolw_ref, pscale_ref, convw_ref, convb_ref,
                            wbd_ref, ba_ref, bi_ref, lam_ref, *rest, ts):
    (qt_ref, kbf_ref, vtb_ref, kmean_ref, kpage_ref, vpage_ref, a_ref, c_ref,
     ptail_ref, ctail_ref, hlast_ref, pbuf, cbuf, hcar) = rest[-14:]
    s = pl.program_id(1)

    @pl.when(s == 0)
    def _():
        pbuf[0:POOL_PAD, :] = jnp.zeros((POOL_PAD, BRANCH_W), F32)
        cbuf[0:CONV_PAD, :] = jnp.zeros((CONV_PAD, BRANCH_W), F32)
        hcar[...] = jnp.zeros((1, BRANCH_W), F32)

    u = _rms(x_ref[...], gpre_ref[...])
    z = jnp.dot(u.astype(BF16), win_ref[...], preferred_element_type=F32)
    bw = BRANCH_W
    xp = z[:, 0:bw]
    k = z[:, 2 * bw:3 * bw]
    xl = z[:, 4 * bw:5 * bw]
    gl = z[:, 5 * bw:6 * bw]

    qt_ref[...] = z[:, bw:2 * bw].T
    kbf_ref[...] = k.astype(BF16)
    kt = k.T
    vt = z[:, 3 * bw:4 * bw].T
    vtb_ref[...] = vt.astype(BF16)
    for n in range(ts // MOBA_BLOCK):
        kmean_ref[n:n + 1, :] = jnp.sum(k[n * MOBA_BLOCK:(n + 1) * MOBA_BLOCK], axis=0, keepdims=True) * (1.0 / MOBA_BLOCK)
    for p in range(ts // PAGE_SIZE):
        kpage_ref[p] = kt[:, p * PAGE_SIZE:(p + 1) * PAGE_SIZE]
        vpage_ref[p] = vt[:, p * PAGE_SIZE:(p + 1) * PAGE_SIZE]

    pos = s * ts + lax.broadcasted_iota(jnp.int32, (ts, 1), 0)

    pbuf[POOL_PAD:POOL_PAD + ts, :] = xp
    wins, cnts = [], []
    for g, w in enumerate(POOL_WINDOWS):
        c0 = g * POOL_GROUP_W
        win = pbuf[POOL_PAD:POOL_PAD + ts, c0:c0 + POOL_GROUP_W]
        for j in range(1, w):
            win = win + pbuf[POOL_PAD - j:POOL_PAD - j + ts, c0:c0 + POOL_GROUP_W]
        wins.append(win)
        cnts.append(jnp.minimum(pos + 1, w).astype(F32))
    a_ref[...] = _pool_mix(wins, xp, cnts, poolw_ref, pscale_ref[...]).astype(a_ref.dtype)
    tail = pbuf[ts:ts + POOL_PAD, :]
    pbuf[0:POOL_PAD, :] = tail
    ptail_ref[...] = tail

    cbuf[CONV_PAD:CONV_PAD + ts, :] = xl
    xc = convb_ref[...] + xl * convw_ref[CONV_W - 1:CONV_W, :]
    for j in range(CONV_W - 1):
        off = CONV_PAD - (CONV_W - 1) + j
        xc = xc + cbuf[off:off + ts, :] * convw_ref[j:j + 1, :]
    ctail = cbuf[ts:ts + CONV_PAD, :]
    cbuf[0:CONV_PAD, :] = ctail
    ctail_ref[...] = ctail

    a, uu = _lru_coeffs(xc, wbd_ref, ba_ref[...], bi_ref[...], lam_ref[...], pos == 0)
    a, uu = _scan8(a, uu)
    carry = jnp.broadcast_to(hcar[...], (SUBLANES, BRANCH_W))
    hs = []
    for g in range(ts // SUBLANES):
        hg = a[g * SUBLANES:(g + 1) * SUBLANES] * carry + uu[g * SUBLANES:(g + 1) * SUBLANES]
        hs.append(hg)
        carry = jnp.broadcast_to(hg[SUBLANES - 1:SUBLANES, :], (SUBLANES, BRANCH_W))
    h = jnp.concatenate(hs, axis=0)
    hcar[...] = carry[0:1, :]
    hlast_ref[...] = carry[0:1, :]
    c_ref[...] = (h * jax.nn.gelu(gl)).astype(c_ref.dtype)


def _const_spec(shape):
    nd = len(shape)
    return pl.BlockSpec(shape, lambda *_: (0,) * nd)


def _mixer_in_prompt(x, n_batch, seq, ts, layer, depth, kv_pages, gpre, win, poolw, pscale, convw, convb,
                     wbd, ba, bi, lam):
    assert ts % MOBA_BLOCK == 0 and seq % ts == 0
    n = n_batch * seq
    nt = seq // ts
    ppt = ts // PAGE_SIZE
    bpt = ts // MOBA_BLOCK
    row = lambda w: pl.BlockSpec((ts, w), lambda b, s: (b * nt + s, 0))
    chan = pl.BlockSpec((None, BRANCH_W, ts), lambda b, s: (b, 0, s))
    per_b = lambda r: pl.BlockSpec((None, r, BRANCH_W), lambda b, s: (b, 0, 0))
    pages = pl.BlockSpec((None, None, ppt, BRANCH_W, PAGE_SIZE), lambda b, s: (layer, b, s, 0, 0))
    page_shape = jax.ShapeDtypeStruct((depth, n_batch, seq // PAGE_SIZE, BRANCH_W, PAGE_SIZE), F32)
    consts = (gpre, win, poolw, pscale, convw, convb, wbd, ba, bi, lam)
    extra = tuple(kv_pages)
    assert all(t.shape == page_shape.shape and t.dtype == page_shape.dtype for t in extra)
    n_in = 1 + len(consts)
    return pl.pallas_call(
        functools.partial(_mixer_in_prompt_kernel, ts=ts),
        grid=(n_batch, nt),
        in_specs=[row(D_MODEL)] + [_const_spec(c.shape) for c in consts]
        + [pl.BlockSpec(memory_space=pl.ANY)] * len(extra),
        out_specs=[chan, row(BRANCH_W), chan,
                   pl.BlockSpec((None, None, bpt, BRANCH_W), lambda b, s: (b, s, 0, 0)),
                   pages, pages, row(BRANCH_W), row(BRANCH_W),
                   per_b(POOL_PAD), per_b(CONV_PAD), per_b(1)],
        out_shape=[jax.ShapeDtypeStruct((n_batch, BRANCH_W, seq), F32),
                   jax.ShapeDtypeStruct((n, BRANCH_W), BF16),
                   jax.ShapeDtypeStruct((n_batch, BRANCH_W, seq), BF16),
                   jax.ShapeDtypeStruct((n_batch, nt, bpt, BRANCH_W), F32),
                   page_shape, page_shape,
                   jax.ShapeDtypeStruct((n, BRANCH_W), BF16),
                   jax.ShapeDtypeStruct((n, BRANCH_W), BF16),
                   jax.ShapeDtypeStruct((n_batch, POOL_PAD, BRANCH_W), F32),
                   jax.ShapeDtypeStruct((n_batch, CONV_PAD, BRANCH_W), F32),
                   jax.ShapeDtypeStruct((n_batch, 1, BRANCH_W), F32)],
        scratch_shapes=[pltpu.VMEM((POOL_PAD + ts, BRANCH_W), F32),
                        pltpu.VMEM((CONV_PAD + ts, BRANCH_W), F32),
                        pltpu.VMEM((1, BRANCH_W), F32)],
        input_output_aliases={n_in + i: 4 + i for i in range(len(extra))},
        compiler_params=_cparams(2),
        name="mixer_in_prompt",
    )(x, *consts, *extra)


def _mixer_in_sample_kernel(x_ref, phist_ref, chist_ref, h0_ref, gpre_ref, win_ref, poolw_ref,
                            pscale_ref, convw_ref, convb_ref, wbd_ref, ba_ref, bi_ref, lam_ref,
                            q_ref, k_ref, v_ref, a_ref, c_ref, ptail_ref, ctail_ref, hlast_ref,
                            pbuf, cbuf, *, n_batch, sq, start):
    rows = n_batch * sq
    u = _rms(x_ref[...], gpre_ref[...])
    z = jnp.dot(u.astype(BF16), win_ref[...], preferred_element_type=F32)
    bw = BRANCH_W
    xp = z[:, 0:bw]
    q_ref[...] = z[:, bw:2 * bw]
    k_ref[...] = z[:, 2 * bw:3 * bw]
    v_ref[...] = z[:, 3 * bw:4 * bw]
    xl = z[:, 4 * bw:5 * bw]
    gl = z[:, 5 * bw:6 * bw]

    pos = start + (lax.broadcasted_iota(jnp.int32, (rows, 1), 0) & (sq - 1))

    pbuf[:, 0:POOL_PAD, :] = phist_ref[...]
    pbuf[:, POOL_PAD:POOL_PAD + sq, :] = xp.reshape(n_batch, sq, bw)
    wins, cnts = [], []
    for g, w in enumerate(POOL_WINDOWS):
        c0 = g * POOL_GROUP_W
        win = pbuf[:, POOL_PAD:POOL_PAD + sq, c0:c0 + POOL_GROUP_W]
        for j in range(1, w):
            win = win + pbuf[:, POOL_PAD - j:POOL_PAD - j + sq, c0:c0 + POOL_GROUP_W]
        wins.append(win.reshape(rows, POOL_GROUP_W))
        cnts.append(jnp.minimum(pos + 1, w).astype(F32))
    a_ref[...] = _pool_mix(wins, xp, cnts, poolw_ref, pscale_ref[...]).astype(a_ref.dtype)
    ptail_ref[...] = pbuf[:, sq:sq + POOL_PAD, :]

    cbuf[:, 0:CONV_PAD, :] = chist_ref[...]
    cbuf[:, CONV_PAD:CONV_PAD + sq, :] = xl.reshape(n_batch, sq, bw)
    xc = xl.reshape(n_batch, sq, bw) * convw_ref[CONV_W - 1:CONV_W, :]
    for j in range(CONV_W - 1):
        off = CONV_PAD - (CONV_W - 1) + j
        xc = xc + cbuf[:, off:off + sq, :] * convw_ref[j:j + 1, :]
    xc = xc.reshape(rows, bw) + convb_ref[...]
    ctail_ref[...] = cbuf[:, sq:sq + CONV_PAD, :]

    a, uu = _lru_coeffs(xc, wbd_ref, ba_ref[...], bi_ref[...], lam_ref[...],
                        (pos == 0) if start == 0 else None)
    a, uu = _scan8(a, uu)
    h = a.reshape(n_batch, sq, bw) * h0_ref[...] + uu.reshape(n_batch, sq, bw)
    hlast_ref[...] = h[:, sq - 1:sq, :]
    c_ref[...] = (h.reshape(rows, bw) * jax.nn.gelu(gl)).astype(c_ref.dtype)


def _mixer_in_sample(x, n_batch, sq, start, phist, chist, h0, gpre, win, poolw, pscale, convw, convb,
                     wbd, ba, bi, lam):
    assert sq == SUBLANES, "sample sequences must fill exactly one sublane group"
    rows = n_batch * sq
    args = (x, phist, chist, h0, gpre, win, poolw, pscale, convw, convb, wbd, ba, bi, lam)
    return pl.pallas_call(
        functools.partial(_mixer_in_sample_kernel, n_batch=n_batch, sq=sq, start=start),
        grid=(1,),
        in_specs=[_const_spec(a.shape) for a in args],
        out_specs=[_const_spec((rows, BRANCH_W))] * 5
        + [_const_spec((n_batch, POOL_PAD, BRANCH_W)), _const_spec((n_batch, CONV_PAD, BRANCH_W)),
           _const_spec((n_batch, 1, BRANCH_W))],
        out_shape=[jax.ShapeDtypeStruct((rows, BRANCH_W), F32)] * 3
        + [jax.ShapeDtypeStruct((rows, BRANCH_W), BF16)] * 2
        + [jax.ShapeDtypeStruct((n_batch, POOL_PAD, BRANCH_W), F32),
           jax.ShapeDtypeStruct((n_batch, CONV_PAD, BRANCH_W), F32),
           jax.ShapeDtypeStruct((n_batch, 1, BRANCH_W), F32)],
        scratch_shapes=[pltpu.VMEM((n_batch, POOL_PAD + sq, BRANCH_W), F32),
                        pltpu.VMEM((n_batch, CONV_PAD + sq, BRANCH_W), F32)],
        compiler_params=_cparams(1),
        name="mixer_in_sample",
    )(*args)


def _topk_rows(g, n_valid, k):
    ridx = lax.broadcasted_iota(jnp.int32, g.shape, 0)
    valid = ridx < n_valid
    rows = []
    for j in range(n_valid):
        gj = g[j:j + 1, :]
        beats = jnp.where(ridx < j, jnp.where(g >= gj, 1.0, 0.0), jnp.where(g > gj, 1.0, 0.0))
        cnt = jnp.sum(jnp.where(valid, beats, 0.0), axis=0, keepdims=True)
        rows.append(cnt < float(k))
    return rows


def _moba_prompt_block(i, qt_ref, k_ref, vt_ref, means_ref, o_ref):
    blk = MOBA_BLOCK
    pair_w = 2 * HEAD_DIM
    nk = (i + 1) * blk
    qt = qt_ref[...]
    ch = lax.broadcasted_iota(jnp.int32, (pair_w, 1), 0)
    key_r = lax.broadcasted_iota(jnp.int32, (blk, blk), 0)
    qry_c = lax.broadcasted_iota(jnp.int32, (blk, blk), 1)
    causal = key_r <= qry_c
    scale = HEAD_DIM ** -0.5 * LOG2_E

    outs = []
    for h in range(2):
        qh = jnp.where((ch >= h * HEAD_DIM) & (ch < (h + 1) * HEAD_DIM), qt, 0.0)
        qs = (qh * scale).astype(BF16)
        sel = _topk_rows(_dot_split(means_ref[...], qh, _NN), i, MOBA_TOPK) if i > MOBA_TOPK else None
        st = jnp.dot(k_ref[0:nk, :], qs, preferred_element_type=F32)
        pieces = []
        for j in range(i):
            sj = st[j * blk:(j + 1) * blk]
            pieces.append(sj if sel is None else jnp.where(sel[j], sj, NEG_INF))
        pieces.append(jnp.where(causal, st[i * blk:], NEG_INF))
        sm = jnp.concatenate(pieces, axis=0) if i else pieces[0]
        m = jnp.max(sm, axis=0, keepdims=True)
        p = jnp.exp2(sm - m)
        l = jnp.sum(p, axis=0, keepdims=True)
        acc = jnp.dot(vt_ref[:, 0:nk], p.astype(BF16), preferred_element_type=F32)
        outs.append(acc / l)

    out_t = jnp.where(ch < HEAD_DIM, outs[0], outs[1])
    o_ref[...] = out_t.T.astype(o_ref.dtype)


def _moba_prompt_kernel(qt_ref, k_ref, vt_ref, means_ref, o_ref, *, nblk):
    i = pl.program_id(2)
    for c in range(nblk):
        pl.when(i == c)(functools.partial(_moba_prompt_block, c, qt_ref, k_ref, vt_ref, means_ref, o_ref))


def _moba_prompt(qt, kbf, vtb, means, n_batch, seq):
    assert seq % MOBA_BLOCK == 0
    nblk = seq // MOBA_BLOCK
    pair_w = 2 * HEAD_DIM
    n_pairs = BRANCH_W // pair_w
    out = pl.pallas_call(
        functools.partial(_moba_prompt_kernel, nblk=nblk),
        grid=(n_batch, n_pairs, nblk),
        in_specs=[pl.BlockSpec((None, pair_w, MOBA_BLOCK), lambda b, hp, i: (b, hp, i)),
                  pl.BlockSpec((None, seq, pair_w), lambda b, hp, i: (b, 0, hp)),
                  pl.BlockSpec((None, pair_w, seq), lambda b, hp, i: (b, hp, 0)),
                  pl.BlockSpec((None, nblk, pair_w), lambda b, hp, i: (b, 0, hp))],
        out_specs=pl.BlockSpec((None, MOBA_BLOCK, pair_w), lambda b, hp, i: (b, i, hp)),
        out_shape=jax.ShapeDtypeStruct((n_batch, seq, BRANCH_W), BF16),
        compiler_params=_cparams(3),
        name="moba_prompt",
    )(qt, kbf.reshape(n_batch, seq, BRANCH_W), vtb, means)
    return out.reshape(n_batch * seq, BRANCH_W)


CHUNK_PAGES = 8


def _moba_sample_kernel(pt_ref, q_ref, kn_ref, vn_ref, ck_ref, cv_ref, o_ref,
                        buf, sem, s_sc, *, layer, n_batch, n_pages, sq):
    b = pl.program_id(0)
    ppb = MOBA_BLOCK // PAGE_SIZE
    nblk = n_pages // ppb
    n_chunks = n_pages // CHUNK_PAGES
    bpc = CHUNK_PAGES // ppb
    nrow = N_HEADS * sq

    def chunk_copies(bb, c, slot):
        src = ck_ref if c < n_chunks else cv_ref
        cc = c % n_chunks
        return [pltpu.make_async_copy(src.at[layer, pt_ref[bb, cc * CHUNK_PAGES + p]],
                                      buf.at[slot, p], sem.at[slot])
                for p in range(CHUNK_PAGES)]

    def start_chunk(bb, c):
        for cp in chunk_copies(bb, c, c % 2):
            cp.start()

    def wait_chunk(bb, c):
        for cp in chunk_copies(bb, c, c % 2):
            cp.wait()

    @pl.when(b == 0)
    def _():
        start_chunk(b, 0)

    qb = q_ref[...]
    qrows = jnp.concatenate([qb] * N_HEADS, axis=0)
    rh = lax.broadcasted_iota(jnp.int32, (nrow, BRANCH_W), 0) >> (sq.bit_length() - 1)
    chh = lax.broadcasted_iota(jnp.int32, (nrow, BRANCH_W), 1) >> (HEAD_DIM.bit_length() - 1)
    head_mask = rh == chh
    qrows = jnp.where(head_mask, qrows, 0.0)
    qs = (qrows * (HEAD_DIM ** -0.5)).astype(BF16)

    def block_of(slot, n):
        return jnp.concatenate([buf[slot, pg] for pg in range(n * ppb, (n + 1) * ppb)], axis=1)

    lane_i = lax.broadcasted_iota(jnp.int32, (nrow, LANES), 1)
    gs = jnp.zeros((nrow, LANES), F32)
    for c in range(n_chunks):
        start_chunk(b, c + 1)
        wait_chunk(b, c)
        for n in range(bpc):
            blk_id = c * bpc + n
            s = jnp.dot(qs, block_of(c % 2, n).astype(BF16), preferred_element_type=F32)
            s_sc[blk_id] = s
            gs = jnp.where(lane_i == blk_id, jnp.sum(s, axis=1, keepdims=True), gs)

    lane = lane_i.astype(F32)
    sel_f = jnp.zeros(gs.shape, F32)
    work = jnp.where(lane_i < nblk, gs, -jnp.inf)
    for _ in range(min(MOBA_TOPK, nblk)):
        mx = jnp.max(work, axis=1, keepdims=True)
        first = jnp.min(jnp.where(work == mx, lane, float(nblk)), axis=1, keepdims=True)
        pick = lane == first
        sel_f = jnp.where(pick, 1.0, sel_f)
        work = jnp.where(pick, -jnp.inf, work)

    s_own = lax.dot_general(qs, kn_ref[...].astype(BF16), _NT, preferred_element_type=F32)
    qi = lax.broadcasted_iota(jnp.int32, (nrow, sq), 0) & (sq - 1)
    kk = lax.broadcasted_iota(jnp.int32, (nrow, sq), 1)
    s_own = jnp.where(kk <= qi, s_own, NEG_INF)
    def masked(blk_id):
        return jnp.where(sel_f[:, blk_id:blk_id + 1] > 0.5, s_sc[blk_id], NEG_INF)

    m_run = masked(0)
    for n in range(1, nblk):
        m_run = jnp.maximum(m_run, masked(n))
    m = jnp.maximum(jnp.max(s_own, axis=1, keepdims=True), jnp.max(m_run, axis=1, keepdims=True))
    p_own = jnp.exp(s_own - m)
    acc = jnp.dot(p_own.astype(BF16), vn_ref[...].astype(BF16), preferred_element_type=F32)

    l_run = jnp.zeros((nrow, MOBA_BLOCK), F32)
    for c in range(n_chunks, 2 * n_chunks):
        if c + 1 < 2 * n_chunks:
            start_chunk(b, c + 1)
        else:
            @pl.when(b + 1 < n_batch)
            def _():
                start_chunk(b + 1, 0)
        wait_chunk(b, c)
        for n in range(bpc):
            p = jnp.exp(masked((c - n_chunks) * bpc + n) - m)
            l_run = l_run + p
            acc = acc + lax.dot_general(p.astype(BF16), block_of(c % 2, n).astype(BF16), _NT,
                                        preferred_element_type=F32)
    l = jnp.sum(p_own, axis=1, keepdims=True) + jnp.sum(l_run, axis=1, keepdims=True)

    full = jnp.where(head_mask, acc / l, 0.0)
    out = full[0:sq]
    for h in range(1, N_HEADS):
        out = out + full[h * sq:(h + 1) * sq]
    o_ref[...] = out.astype(o_ref.dtype)


def _moba_sample(q, k_new, v_new, cache_k, cache_v, page_table, layer, n_batch, sq):
    n_pages = page_table.shape[1]
    assert (n_pages * PAGE_SIZE) % MOBA_BLOCK == 0, "own block must hold only the new rows"
    assert n_pages % CHUNK_PAGES == 0 and (n_pages // CHUNK_PAGES) % 2 == 0
    nblk = n_pages * PAGE_SIZE // MOBA_BLOCK
    assert nblk <= LANES, "block scores are kept one lane per block"
    depth, n_pool = cache_k.shape[0], cache_k.shape[1]
    ck = cache_k.transpose(0, 1, 3, 4, 2).reshape(depth, n_pool, BRANCH_W, PAGE_SIZE)
    cv = cache_v.transpose(0, 1, 3, 4, 2).reshape(depth, n_pool, BRANCH_W, PAGE_SIZE)
    q3, k3, v3 = (t.reshape(n_batch, sq, BRANCH_W) for t in (q, k_new, v_new))
    row = pl.BlockSpec((None, sq, BRANCH_W), lambda b, pt: (b, 0, 0))
    hbm = pl.BlockSpec(memory_space=pl.ANY)
    out = pl.pallas_call(
        functools.partial(_moba_sample_kernel, layer=layer, n_batch=n_batch, n_pages=n_pages, sq=sq),
        grid_spec=pltpu.PrefetchScalarGridSpec(
            num_scalar_prefetch=1,
            grid=(n_batch,),
            in_specs=[row, row, row, hbm, hbm],
            out_specs=row,
            scratch_shapes=[pltpu.VMEM((2, CHUNK_PAGES, BRANCH_W, PAGE_SIZE), F32),
                            pltpu.SemaphoreType.DMA((2,)),
                            pltpu.VMEM((nblk, N_HEADS * sq, MOBA_BLOCK), F32)]),
        out_shape=jax.ShapeDtypeStruct((n_batch, sq, BRANCH_W), BF16),
        compiler_params=_cparams(1),
        name="moba_sample",
    )(page_table, q3, k3, v3, ck, cv)
    return out.reshape(n_batch * sq, BRANCH_W)


def _merge_kernel(x_ref, a_ref, b_ref, c_ref, gpre_ref, gpost_ref, wg_ref, proj_ref, wo_ref, o_ref):
    x = x_ref[...]
    u = _rms(x, gpre_ref[...]).astype(BF16)
    gates = jax.nn.sigmoid(jnp.dot(u, wg_ref[...], preferred_element_type=F32))
    merged = None
    for n, br in enumerate((a_ref, b_ref, c_ref)):
        t = gates[:, n * D_MODEL:(n + 1) * D_MODEL] * jnp.dot(br[...], proj_ref[n], preferred_element_type=F32)
        merged = t if merged is None else merged + t
    m = jnp.dot(merged.astype(BF16), wo_ref[...], preferred_element_type=F32)
    o_ref[...] = x + _rms(m, gpost_ref[...])


def _merge(x, a, b, c, gpre, gpost, wg, proj, wo, tm):
    n = x.shape[0]
    row = lambda w: pl.BlockSpec((tm, w), lambda i: (i, 0))
    consts = (gpre, gpost, wg, proj, wo)
    return pl.pallas_call(
        _merge_kernel,
        grid=(n // tm,),
        in_specs=[row(D_MODEL)] + [row(BRANCH_W)] * 3 + [_const_spec(t.shape) for t in consts],
        out_specs=row(D_MODEL),
        out_shape=jax.ShapeDtypeStruct((n, D_MODEL), F32),
        compiler_params=_cparams(1),
        name="merge",
    )(x, a, b, c, *consts)


def _ffn_dense_kernel(x_ref, gpre_ref, gpost_ref, wg_ref, wu_ref, wd_ref, o_ref, u_sc, acc_sc):
    f = pl.program_id(1)

    @pl.when(f == 0)
    def _():
        u_sc[...] = _rms(x_ref[...], gpre_ref[...]).astype(BF16)
        acc_sc[...] = jnp.zeros(acc_sc.shape, F32)

    u = u_sc[...]
    hg = jnp.dot(u, wg_ref[...], preferred_element_type=F32)
    hu = jnp.dot(u, wu_ref[...], preferred_element_type=F32)
    acc_sc[...] += jnp.dot((jax.nn.silu(hg) * hu).astype(BF16), wd_ref[...], preferred_element_type=F32)

    @pl.when(f == pl.num_programs(1) - 1)
    def _():
        o_ref[...] = x_ref[...] + _rms(acc_sc[...], gpost_ref[...])


def _ffn_dense(x, gpre, gpost, wg, wu, wd, tm, fc):
    n = x.shape[0]
    dff = wg.shape[1]
    row = pl.BlockSpec((tm, D_MODEL), lambda i, f: (i, 0))
    return pl.pallas_call(
        _ffn_dense_kernel,
        grid=(n // tm, dff // fc),
        in_specs=[row, _const_spec(gpre.shape), _const_spec(gpost.shape),
                  pl.BlockSpec((D_MODEL, fc), lambda i, f: (0, f)),
                  pl.BlockSpec((D_MODEL, fc), lambda i, f: (0, f)),
                  pl.BlockSpec((fc, D_MODEL), lambda i, f: (f, 0))],
        out_specs=row,
        out_shape=jax.ShapeDtypeStruct((n, D_MODEL), F32),
        scratch_shapes=[pltpu.VMEM((tm, D_MODEL), BF16), pltpu.VMEM((tm, D_MODEL), F32)],
        compiler_params=_cparams(2),
        name="ffn_dense",
    )(x, gpre, gpost, wg, wu, wd)


def _router_gate(u, router_ref):
    logits = _dot_split(u, router_ref[...], _NN)
    lane = lax.broadcasted_iota(jnp.int32, logits.shape, 1)
    logits = jnp.where(lane < N_EXPERTS, logits, -jnp.inf)
    m1 = jnp.max(logits, axis=1, keepdims=True)
    i1 = jnp.min(jnp.where(logits == m1, lane, LANES), axis=1, keepdims=True)
    rest = jnp.where(lane == i1, -jnp.inf, logits)
    m2 = jnp.max(rest, axis=1, keepdims=True)
    i2 = jnp.min(jnp.where(rest == m2, lane, LANES), axis=1, keepdims=True)
    e2 = jnp.exp(m2 - m1)
    w1 = 1.0 / (1.0 + e2)
    w2 = e2 / (1.0 + e2)
    gate = jnp.where(lane == i1, w1, 0.0) + jnp.where(lane == i2, w2, 0.0)
    chosen = jnp.where((lane == i1) | (lane == i2), 1.0, 0.0)
    return gate, chosen


MOE_CHUNK = 128
MOE_SLAB = 256


def _moe_kernel(x_ref, gpre_ref, gpost_ref, router_ref, wg_ref, wu_ref, wd_ref, o_ref,
                u_sc, gate_sc, rank_sc, rankt_sc, cnt_sc, xs_sc, y_sc, *, tm):
    e = pl.program_id(1)
    f = pl.program_id(2)
    nf = pl.num_programs(2)
    ck = MOE_CHUNK

    @pl.when((e == 0) & (f == 0))
    def _():
        u = _rms(x_ref[...], gpre_ref[...])
        u_sc[...] = u.astype(BF16)
        gate, chosen = _router_gate(u, router_ref)
        gate_sc[...] = gate
        earlier = (lax.broadcasted_iota(jnp.int32, (tm, tm), 1) < lax.broadcasted_iota(jnp.int32, (tm, tm), 0))
        rank = jnp.dot(jnp.where(earlier, 1.0, 0.0).astype(BF16), chosen.astype(BF16), preferred_element_type=F32)
        rank = jnp.where(chosen > 0.5, rank, -1.0)
        rank_sc[...] = rank
        rankt_sc[...] = rank.T[0:SUBLANES, :]
        cnt_sc[...] = jnp.sum(chosen, axis=0, keepdims=True)
        o_ref[...] = jnp.zeros(o_ref.shape, F32)

    lane1 = lax.broadcasted_iota(jnp.int32, (1, LANES), 1)
    cnt = jnp.sum(jnp.where(lane1 == e, cnt_sc[...], 0.0))
    n_chunks = jnp.ceil(cnt * (1.0 / ck)).astype(jnp.int32)

    @pl.when(f == 0)
    def _():
        rank_row = rankt_sc[pl.ds(e, 1), :]
        slot = lax.broadcasted_iota(jnp.int32, (ck, 1), 0).astype(F32)

        def compact(c, carry):
            base = pl.multiple_of(c * ck, ck)
            onehot = jnp.where(rank_row - (c * ck).astype(F32) == slot, 1.0, 0.0).astype(BF16)
            xs_sc[pl.ds(base, ck), :] = jnp.dot(onehot, u_sc[...], preferred_element_type=F32).astype(BF16)
            return carry

        lax.fori_loop(0, n_chunks, compact, 0)

    def expert(c, carry):
        base = pl.multiple_of(c * ck, ck)
        xs = xs_sc[pl.ds(base, ck), :]
        hg = jnp.dot(xs, wg_ref[...], preferred_element_type=F32)
        hu = jnp.dot(xs, wu_ref[...], preferred_element_type=F32)
        y = jnp.dot((jax.nn.silu(hg) * hu).astype(BF16), wd_ref[...], preferred_element_type=F32)

        @pl.when(f == 0)
        def _():
            y_sc[pl.ds(base, ck), :] = y

        @pl.when(f > 0)
        def _():
            y_sc[pl.ds(base, ck), :] += y

        return carry

    lax.fori_loop(0, n_chunks, expert, 0)

    @pl.when(f == nf - 1)
    def _():
        lane = lax.broadcasted_iota(jnp.int32, (MOE_SLAB, LANES), 1)
        slot = lax.broadcasted_iota(jnp.int32, (1, ck), 1).astype(F32)

        def scatter(c, carry):
            base = pl.multiple_of(c * ck, ck)
            yb = y_sc[pl.ds(base, ck), :].astype(BF16)
            for s0 in range(0, tm, MOE_SLAB):
                rows = slice(s0, s0 + MOE_SLAB)
                rank_col = jnp.sum(jnp.where(lane == e, rank_sc[rows, :], 0.0), axis=1, keepdims=True)
                gate_col = jnp.sum(jnp.where(lane == e, gate_sc[rows, :], 0.0), axis=1, keepdims=True)
                onehot_t = jnp.where(rank_col - (c * ck).astype(F32) == slot, 1.0, 0.0).astype(BF16)
                o_ref[rows, :] += gate_col * jnp.dot(onehot_t, yb, preferred_element_type=F32)
            return carry

        lax.fori_loop(0, n_chunks, scatter, 0)

    @pl.when((e == pl.num_programs(1) - 1) & (f == nf - 1))
    def _():
        o_ref[...] = x_ref[...] + _rms(o_ref[...], gpost_ref[...])


def _moe(x, gpre, gpost, router_pad, wg, wu, wd, tm, fc):
    n = x.shape[0]
    n_exp, _, dff = wg.shape
    assert n_exp <= SUBLANES and tm % MOE_SLAB == 0 and MOE_SLAB % MOE_CHUNK == 0
    cap = pl.cdiv(tm, MOE_CHUNK) * MOE_CHUNK
    row = pl.BlockSpec((tm, D_MODEL), lambda i, e, f: (i, 0))
    return pl.pallas_call(
        functools.partial(_moe_kernel, tm=tm),
        grid=(n // tm, n_exp, dff // fc),
        in_specs=[row, _const_spec(gpre.shape), _const_spec(gpost.shape), _const_spec(router_pad.shape),
                  pl.BlockSpec((None, D_MODEL, fc), lambda i, e, f: (e, 0, f)),
                  pl.BlockSpec((None, D_MODEL, fc), lambda i, e, f: (e, 0, f)),
                  pl.BlockSpec((None, fc, D_MODEL), lambda i, e, f: (e, f, 0))],
        out_specs=row,
        out_shape=jax.ShapeDtypeStruct((n, D_MODEL), F32),
        scratch_shapes=[pltpu.VMEM((tm, D_MODEL), BF16), pltpu.VMEM((tm, LANES), F32),
                        pltpu.VMEM((tm, LANES), F32), pltpu.VMEM((SUBLANES, tm), F32),
                        pltpu.VMEM((1, LANES), F32),
                        pltpu.VMEM((cap, D_MODEL), BF16), pltpu.VMEM((cap, D_MODEL), F32)],
        compiler_params=_cparams(3),
        name="moe",
    )(x, gpre, gpost, router_pad, wg, wu, wd)


def _block_diag_halves(w_a, w_i):
    half_blocks = LRU_BLOCKS // 2

    def bd(w4):
        eye = jnp.eye(half_blocks, dtype=w4.dtype)
        return jnp.einsum('hcd,hg->hcgd', w4, eye).reshape(half_blocks * LRU_BLOCK_W, half_blocks * LRU_BLOCK_W)

    halves = [jnp.concatenate([bd(w_a[i * half_blocks:(i + 1) * half_blocks]),
                               bd(w_i[i * half_blocks:(i + 1) * half_blocks])], axis=1) for i in range(2)]
    return jnp.stack(halves).astype(BF16)


def _row(v):
    return v.reshape(1, -1)


def _pad_hist(h, pad_rows):
    return jnp.pad(h, ((0, 0), (pad_rows - h.shape[1], 0), (0, 0)))


PROMPT_TS = 512
ROW_TM = 512
FFN_FC = 1408
MOE_FC = 1792
MOE_TM = 1024


def kernel(x_prompt, x_sample, cache_k, cache_v, page_table, state_pool, state_conv, state_h, norm_mix_pre, norm_mix_post, norm_ffn_pre, norm_ffn_post, w_in, pool_w, pool_scale, conv_w, conv_b, lru_wa, lru_ba, lru_wi, lru_bi, lru_lambda, proj_pool, proj_attn, proj_lru, w_o, ffn_w_gate, ffn_w_up, ffn_w_down, router_w, moe_w_gate, moe_w_up, moe_w_down):
    depth = w_in.shape[0]
    bp, sp, _ = x_prompt.shape
    bs, ss, _ = x_sample.shape
    past_len = page_table.shape[1] * PAGE_SIZE
    xp = x_prompt.reshape(bp * sp, D_MODEL)
    xs = x_sample.reshape(bs * ss, D_MODEL)
    tm_s = bs * ss

    outs = {name: [] for name in ("ks", "vs", "pp", "ps", "cp", "cs", "hp", "hs")}
    kv_pages = tuple(jnp.zeros((depth, bp, sp // PAGE_SIZE, BRANCH_W, PAGE_SIZE), F32) for _ in range(2))
    for l in range(depth):
        w_qkv = w_in[l, :, :D_QKV].astype(BF16)
        w_gates = w_in[l, :, D_QKV:].astype(BF16)
        mix_consts = (_row(norm_mix_pre[l]), w_qkv, pool_w[l].astype(BF16), _row(pool_scale[l]),
                      conv_w[l], _row(conv_b[l]), _block_diag_halves(lru_wa[l], lru_wi[l]),
                      _row(lru_ba[l]), _row(lru_bi[l]), _row(lru_lambda[l]))
        proj = jnp.stack([proj_pool[l], proj_attn[l], proj_lru[l]]).astype(BF16)
        merge_consts = (_row(norm_mix_pre[l]), _row(norm_mix_post[l]), w_gates, proj, w_o[l].astype(BF16))

        qt_p, kbf_p, vtb_p, kmean_p, kpages, vpages, a_p, c_p, pt_p, ct_p, h_p = _mixer_in_prompt(
            xp, bp, sp, PROMPT_TS, l, depth, kv_pages, *mix_consts)
        kv_pages = (kpages, vpages)
        b_p = _moba_prompt(qt_p, kbf_p, vtb_p, kmean_p.reshape(bp, sp // MOBA_BLOCK, BRANCH_W), bp, sp)
        xp = _merge(xp, a_p, b_p, c_p, *merge_consts, ROW_TM)

        q_s, k_s, v_s, a_s, c_s, pt_s, ct_s, h_s = _mixer_in_sample(
            xs, bs, ss, past_len, _pad_hist(state_pool[l], POOL_PAD), _pad_hist(state_conv[l], CONV_PAD),
            state_h[l].reshape(bs, 1, BRANCH_W), *mix_consts)
        b_s = _moba_sample(q_s, k_s, v_s, cache_k, cache_v, page_table, l, bs, ss)
        xs = _merge(xs, a_s, b_s, c_s, *merge_consts, tm_s)

        gpre, gpost = _row(norm_ffn_pre[l]), _row(norm_ffn_post[l])
        j = l // 2
        if l % 2 == 0:
            wg, wu, wd = (w[j].astype(BF16) for w in (ffn_w_gate, ffn_w_up, ffn_w_down))
            xp = _ffn_dense(xp, gpre, gpost, wg, wu, wd, ROW_TM, FFN_FC)
            xs = _ffn_dense(xs, gpre, gpost, wg, wu, wd, tm_s, FFN_FC)
        else:
            wg, wu, wd = (w[j].astype(BF16) for w in (moe_w_gate, moe_w_up, moe_w_down))
            router_pad = jnp.pad(router_w[j], ((0, 0), (0, LANES - N_EXPERTS)))
            xp = _moe(xp, gpre, gpost, router_pad, wg, wu, wd, MOE_TM, MOE_FC)
            xs = _moe(xs, gpre, gpost, router_pad, wg, wu, wd, tm_s, MOE_FC)

        outs["ks"].append(k_s.reshape(bs, ss, N_HEADS, HEAD_DIM))
        outs["vs"].append(v_s.reshape(bs, ss, N_HEADS, HEAD_DIM))
        outs["pp"].append(pt_p[:, POOL_PAD - POOL_HIST:])
        outs["ps"].append(pt_s[:, POOL_PAD - POOL_HIST:])
        outs["cp"].append(ct_p[:, CONV_PAD - (CONV_W - 1):])
        outs["cs"].append(ct_s[:, CONV_PAD - (CONV_W - 1):])
        outs["hp"].append(h_p.reshape(bp, BRANCH_W))
        outs["hs"].append(h_s.reshape(bs, BRANCH_W))

    st = {k: jnp.stack(v) for k, v in outs.items()}
    kp, vp = (t.reshape(depth, bp, sp // PAGE_SIZE, N_HEADS, HEAD_DIM, PAGE_SIZE).transpose(0, 1, 2, 5, 3, 4)
              for t in kv_pages)
    return (xp.reshape(bp, sp, D_MODEL), xs.reshape(bs, ss, D_MODEL),
            kp, vp, st["ks"], st["vs"], st["pp"], st["ps"], st["cp"], st["cs"], st["hp"], st["hs"])
```

```python
import functools

import jax
import jax.numpy as jnp
from jax import lax
from jax.experimental import pallas as pl
from jax.experimental.pallas import tpu as pltpu

F32 = jnp.float32
BF16 = jnp.bfloat16

D_MODEL = 1024
BRANCH_W = 512
POOL_WINDOWS = (2, 4, 8, 16)
POOL_GROUP_W = BRANCH_W // len(POOL_WINDOWS)
POOL_HIST = max(POOL_WINDOWS) - 1
POOL_PAD = 16
N_HEADS = 8
HEAD_DIM = BRANCH_W // N_HEADS
MOBA_BLOCK = 256
MOBA_TOPK = 3
PAGE_SIZE = 128
LRU_BLOCKS = 8
LRU_BLOCK_W = BRANCH_W // LRU_BLOCKS
CONV_W = 4
CONV_PAD = 8
LRU_C = 8.0
N_EXPERTS = 8
TOP_K = 2
EPS = 1e-6
NEG_INF = -1e30
LOG2_E = 1.4426950408889634
D_QKV = 6 * BRANCH_W
SUBLANES = 8
BF16_SUBLANES = 16
LANES = 128
VMEM_LIMIT = 56 * 1024 * 1024


def _cparams(n_axes):
    return pltpu.CompilerParams(
        dimension_semantics=("arbitrary",) * n_axes, vmem_limit_bytes=VMEM_LIMIT)


def _rms(x, g):
    return x * lax.rsqrt(jnp.mean(x * x, axis=-1, keepdims=True) + EPS) * g


def _split_bf16(x):
    hi = x.astype(BF16)
    lo = (x - hi.astype(F32)).astype(BF16)
    return hi, lo


def _dot_split(a, b, dims):
    a_hi, a_lo = _split_bf16(a)
    b_hi, b_lo = _split_bf16(b)
    d = functools.partial(lax.dot_general, dimension_numbers=dims, preferred_element_type=F32)
    return d(a_hi, b_hi) + (d(a_hi, b_lo) + d(a_lo, b_hi))


_NN = (((1,), (0,)), ((), ()))
_NT = (((1,), (1,)), ((), ()))


def _sigmoid(x):
    return 0.5 * jnp.tanh(0.5 * x) + 0.5


def _softplus(x):
    return jnp.maximum(x, 0.0) + jnp.log1p(jnp.exp(-jnp.abs(x)))


def _lru_coeffs(xc, wbd_ref, ba, bi, lam, first_row_is_start):
    xcb = xc.astype(BF16)
    half = BRANCH_W // 2
    g0 = jnp.dot(xcb[:, :half], wbd_ref[0], preferred_element_type=F32)
    g1 = jnp.dot(xcb[:, half:], wbd_ref[1], preferred_element_type=F32)
    pre_a = jnp.concatenate([g0[:, :half], g1[:, :half]], axis=1)
    pre_i = jnp.concatenate([g0[:, half:], g1[:, half:]], axis=1)
    r = _sigmoid(pre_a + ba)
    i = _sigmoid(pre_i + bi)
    log_a = (-LRU_C) * r * _softplus(-lam)
    a = jnp.exp(log_a)
    t = jnp.tanh(log_a)
    mult = jnp.sqrt(-2.0 * t) * lax.rsqrt(1.0 - t)
    if first_row_is_start is not None:
        mult = jnp.where(first_row_is_start, 1.0, mult)
    return a, mult * i * xc


def _scan8(a, u):
    rows, width = a.shape
    a = a.reshape(rows // SUBLANES, SUBLANES, width)
    u = u.reshape(rows // SUBLANES, SUBLANES, width)
    r8 = lax.broadcasted_iota(jnp.int32, a.shape, 1)
    for d in (1, 2, 4):
        a_s = pltpu.roll(a, d, axis=1)
        u_s = pltpu.roll(u, d, axis=1)
        m = r8 >= d
        u = jnp.where(m, u + a * u_s, u)
        a = jnp.where(m, a * a_s, a)
    return a.reshape(rows, width), u.reshape(rows, width)


def _pool_mix(win_groups, xp, cnt_groups, poolw_ref, pscale):
    outs = []
    for g in range(len(POOL_WINDOWS)):
        c0 = g * POOL_GROUP_W
        pooled = win_groups[g] / cnt_groups[g] - xp[:, c0:c0 + POOL_GROUP_W]
        outs.append(jnp.dot(pooled.astype(BF16), poolw_ref[g], preferred_element_type=F32))
    return jnp.concatenate(outs, axis=1) * pscale


def _mixer_in_prompt_kernel(x_ref, gpre_ref, win_ref, poolw_ref, pscale_ref, convw_ref, convb_ref,
                            wbd_ref, ba_ref, bi_ref, lam_ref, *rest, ts):
    (qt_ref, kbf_ref, vtb_ref, kmean_ref, kpage_ref, vpage_ref, a_ref, c_ref,
     ptail_ref, ctail_ref, hlast_ref, pbuf, cbuf, hcar) = rest[-14:]
    s = pl.program_id(1)

    @pl.when(s == 0)
    def _():
        pbuf[0:POOL_PAD, :] = jnp.zeros((POOL_PAD, BRANCH_W), F32)
        cbuf[0:CONV_PAD, :] = jnp.zeros((CONV_PAD, BRANCH_W), F32)
        hcar[...] = jnp.zeros((1, BRANCH_W), F32)

    u = _rms(x_ref[...], gpre_ref[...])
    z = jnp.dot(u.astype(BF16), win_ref[...], preferred_element_type=F32)
    bw = BRANCH_W
    xp = z[:, 0:bw]
    k = z[:, 2 * bw:3 * bw]
    xl = z[:, 4 * bw:5 * bw]
    gl = z[:, 5 * bw:6 * bw]

    qt_ref[...] = z[:, bw:2 * bw].T
    kbf_ref[...] = k.astype(BF16)
    kt = k.T
    vt = z[:, 3 * bw:4 * bw].T
    vtb_ref[...] = vt.astype(BF16)
    for n in range(ts // MOBA_BLOCK):
        kmean_ref[n:n + 1, :] = jnp.sum(k[n * MOBA_BLOCK:(n + 1) * MOBA_BLOCK], axis=0, keepdims=True) * (1.0 / MOBA_BLOCK)
    for p in range(ts // PAGE_SIZE):
        kpage_ref[p] = kt[:, p * PAGE_SIZE:(p + 1) * PAGE_SIZE]
        vpage_ref[p] = vt[:, p * PAGE_SIZE:(p + 1) * PAGE_SIZE]

    pos = s * ts + lax.broadcasted_iota(jnp.int32, (ts, 1), 0)

    pbuf[POOL_PAD:POOL_PAD + ts, :] = xp
    wins, cnts = [], []
    for g, w in enumerate(POOL_WINDOWS):
        c0 = g * POOL_GROUP_W
        win = pbuf[:, c0:c0 + POOL_GROUP_W]
        d = 1
        while d < w:
            win = win + pltpu.roll(win, d, axis=0)
            d *= 2
        wins.append(win[POOL_PAD:])
        cnts.append(jnp.minimum(pos + 1, w).astype(F32))
    a_ref[...] = _pool_mix(wins, xp, cnts, poolw_ref, pscale_ref[...]).astype(a_ref.dtype)
    tail = pbuf[ts:ts + POOL_PAD, :]
    pbuf[0:POOL_PAD, :] = tail
    ptail_ref[...] = tail

    cbuf[CONV_PAD:CONV_PAD + ts, :] = xl
    xc = convb_ref[...] + xl * convw_ref[CONV_W - 1:CONV_W, :]
    for j in range(CONV_W - 1):
        off = CONV_PAD - (CONV_W - 1) + j
        xc = xc + cbuf[off:off + ts, :] * convw_ref[j:j + 1, :]
    ctail = cbuf[ts:ts + CONV_PAD, :]
    cbuf[0:CONV_PAD, :] = ctail
    ctail_ref[...] = ctail

    a, uu = _lru_coeffs(xc, wbd_ref, ba_ref[...], bi_ref[...], lam_ref[...], pos == 0)
    a, uu = _scan8(a, uu)
    carry = jnp.broadcast_to(hcar[...], (SUBLANES, BRANCH_W))
    hs = []
    for g in range(ts // SUBLANES):
        hg = a[g * SUBLANES:(g + 1) * SUBLANES] * carry + uu[g * SUBLANES:(g + 1) * SUBLANES]
        hs.append(hg)
        carry = jnp.broadcast_to(hg[SUBLANES - 1:SUBLANES, :], (SUBLANES, BRANCH_W))
    h = jnp.concatenate(hs, axis=0)
    hcar[...] = carry[0:1, :]
    hlast_ref[...] = carry[0:1, :]
    c_ref[...] = (h * jax.nn.gelu(gl)).astype(c_ref.dtype)


def _const_spec(shape):
    nd = len(shape)
    return pl.BlockSpec(shape, lambda *_: (0,) * nd)


def _mixer_in_prompt(x, n_batch, seq, ts, layer, depth, kv_pages, gpre, win, poolw, pscale, convw, convb,
                     wbd, ba, bi, lam):
    assert ts % MOBA_BLOCK == 0 and seq % ts == 0
    n = n_batch * seq
    nt = seq // ts
    ppt = ts // PAGE_SIZE
    bpt = ts // MOBA_BLOCK
    row = lambda w: pl.BlockSpec((ts, w), lambda b, s: (b * nt + s, 0))
    chan = pl.BlockSpec((None, BRANCH_W, ts), lambda b, s: (b, 0, s))
    per_b = lambda r: pl.BlockSpec((None, r, BRANCH_W), lambda b, s: (b, 0, 0))
    pages = pl.BlockSpec((None, None, ppt, BRANCH_W, PAGE_SIZE), lambda b, s: (layer, b, s, 0, 0))
    page_shape = jax.ShapeDtypeStruct((depth, n_batch, seq // PAGE_SIZE, BRANCH_W, PAGE_SIZE), F32)
    consts = (gpre, win, poolw, pscale, convw, convb, wbd, ba, bi, lam)
    extra = tuple(kv_pages)
    assert all(t.shape == page_shape.shape and t.dtype == page_shape.dtype for t in extra)
    n_in = 1 + len(consts)
    return pl.pallas_call(
        functools.partial(_mixer_in_prompt_kernel, ts=ts),
        grid=(n_batch, nt),
        in_specs=[row(D_MODEL)] + [_const_spec(c.shape) for c in consts]
        + [pl.BlockSpec(memory_space=pl.ANY)] * len(extra),
        out_specs=[chan, row(BRANCH_W), chan,
                   pl.BlockSpec((None, None, bpt, BRANCH_W), lambda b, s: (b, s, 0, 0)),
                   pages, pages, row(BRANCH_W), row(BRANCH_W),
                   per_b(POOL_PAD), per_b(CONV_PAD), per_b(1)],
        out_shape=[jax.ShapeDtypeStruct((n_batch, BRANCH_W, seq), F32),
                   jax.ShapeDtypeStruct((n, BRANCH_W), BF16),
                   jax.ShapeDtypeStruct((n_batch, BRANCH_W, seq), BF16),
                   jax.ShapeDtypeStruct((n_batch, nt, bpt, BRANCH_W), F32),
                   page_shape, page_shape,
                   jax.ShapeDtypeStruct((n, BRANCH_W), BF16),
                   jax.ShapeDtypeStruct((n, BRANCH_W), BF16),
                   jax.ShapeDtypeStruct((n_batch, POOL_PAD, BRANCH_W), F32),
                   jax.ShapeDtypeStruct((n_batch, CONV_PAD, BRANCH_W), F32),
                   jax.ShapeDtypeStruct((n_batch, 1, BRANCH_W), F32)],
        scratch_shapes=[pltpu.VMEM((POOL_PAD + ts, BRANCH_W), F32),
                        pltpu.VMEM((CONV_PAD + ts, BRANCH_W), F32),
                        pltpu.VMEM((1, BRANCH_W), F32)],
        input_output_aliases={n_in + i: 4 + i for i in range(len(extra))},
        compiler_params=_cparams(2),
        name="mixer_in_prompt",
    )(x, *consts, *extra)


def _mixer_in_sample_kernel(x_ref, phist_ref, chist_ref, h0_ref, gpre_ref, win_ref, poolw_ref,
                            pscale_ref, convw_ref, convb_ref, wbd_ref, ba_ref, bi_ref, lam_ref,
                            q_ref, k_ref, v_ref, a_ref, c_ref, ptail_ref, ctail_ref, hlast_ref,
                            pbuf, cbuf, *, n_batch, sq, start):
    rows = n_batch * sq
    u = _rms(x_ref[...], gpre_ref[...])
    z = jnp.dot(u.astype(BF16), win_ref[...], preferred_element_type=F32)
    bw = BRANCH_W
    xp = z[:, 0:bw]
    q_ref[...] = z[:, bw:2 * bw]
    k_ref[...] = z[:, 2 * bw:3 * bw]
    v_ref[...] = z[:, 3 * bw:4 * bw]
    xl = z[:, 4 * bw:5 * bw]
    gl = z[:, 5 * bw:6 * bw]

    pos = start + (lax.broadcasted_iota(jnp.int32, (rows, 1), 0) & (sq - 1))

    pbuf[:, 0:POOL_PAD, :] = phist_ref[...]
    pbuf[:, POOL_PAD:POOL_PAD + sq, :] = xp.reshape(n_batch, sq, bw)
    wins, cnts = [], []
    for g, w in enumerate(POOL_WINDOWS):
        c0 = g * POOL_GROUP_W
        win = pbuf[:, POOL_PAD:POOL_PAD + sq, c0:c0 + POOL_GROUP_W]
        for j in range(1, w):
            win = win + pbuf[:, POOL_PAD - j:POOL_PAD - j + sq, c0:c0 + POOL_GROUP_W]
        wins.append(win.reshape(rows, POOL_GROUP_W))
        cnts.append(jnp.minimum(pos + 1, w).astype(F32))
    a_ref[...] = _pool_mix(wins, xp, cnts, poolw_ref, pscale_ref[...]).astype(a_ref.dtype)
    ptail_ref[...] = pbuf[:, sq:sq + POOL_PAD, :]

    cbuf[:, 0:CONV_PAD, :] = chist_ref[...]
    cbuf[:, CONV_PAD:CONV_PAD + sq, :] = xl.reshape(n_batch, sq, bw)
    xc = xl.reshape(n_batch, sq, bw) * convw_ref[CONV_W - 1:CONV_W, :]
    for j in range(CONV_W - 1):
        off = CONV_PAD - (CONV_W - 1) + j
        xc = xc + cbuf[:, off:off + sq, :] * convw_ref[j:j + 1, :]
    xc = xc.reshape(rows, bw) + convb_ref[...]
    ctail_ref[...] = cbuf[:, sq:sq + CONV_PAD, :]

    a, uu = _lru_coeffs(xc, wbd_ref, ba_ref[...], bi_ref[...], lam_ref[...],
                        (pos == 0) if start == 0 else None)
    a, uu = _scan8(a, uu)
    h = a.reshape(n_batch, sq, bw) * h0_ref[...] + uu.reshape(n_batch, sq, bw)
    hlast_ref[...] = h[:, sq - 1:sq, :]
    c_ref[...] = (h.reshape(rows, bw) * jax.nn.gelu(gl)).astype(c_ref.dtype)


def _mixer_in_sample(x, n_batch, sq, start, phist, chist, h0, gpre, win, poolw, pscale, convw, convb,
                     wbd, ba, bi, lam):
    assert sq == SUBLANES, "sample sequences must fill exactly one sublane group"
    rows = n_batch * sq
    args = (x, phist, chist, h0, gpre, win, poolw, pscale, convw, convb, wbd, ba, bi, lam)
    return pl.pallas_call(
        functools.partial(_mixer_in_sample_kernel, n_batch=n_batch, sq=sq, start=start),
        grid=(1,),
        in_specs=[_const_spec(a.shape) for a in args],
        out_specs=[_const_spec((rows, BRANCH_W))] * 5
        + [_const_spec((n_batch, POOL_PAD, BRANCH_W)), _const_spec((n_batch, CONV_PAD, BRANCH_W)),
           _const_spec((n_batch, 1, BRANCH_W))],
        out_shape=[jax.ShapeDtypeStruct((rows, BRANCH_W), F32)] * 3
        + [jax.ShapeDtypeStruct((rows, BRANCH_W), BF16)] * 2
        + [jax.ShapeDtypeStruct((n_batch, POOL_PAD, BRANCH_W), F32),
           jax.ShapeDtypeStruct((n_batch, CONV_PAD, BRANCH_W), F32),
           jax.ShapeDtypeStruct((n_batch, 1, BRANCH_W), F32)],
        scratch_shapes=[pltpu.VMEM((n_batch, POOL_PAD + sq, BRANCH_W), F32),
                        pltpu.VMEM((n_batch, CONV_PAD + sq, BRANCH_W), F32)],
        compiler_params=_cparams(1),
        name="mixer_in_sample",
    )(*args)


def _topk_rows(g, n_valid, k):
    ridx = lax.broadcasted_iota(jnp.int32, g.shape, 0)
    valid = ridx < n_valid
    rows = []
    for j in range(n_valid):
        gj = g[j:j + 1, :]
        beats = jnp.where(ridx < j, jnp.where(g >= gj, 1.0, 0.0), jnp.where(g > gj, 1.0, 0.0))
        cnt = jnp.sum(jnp.where(valid, beats, 0.0), axis=0, keepdims=True)
        rows.append(cnt < float(k))
    return rows


def _moba_prompt_block(i, qt_ref, k_ref, vt_ref, means_ref, o_ref):
    blk = MOBA_BLOCK
    pair_w = 2 * HEAD_DIM
    nk = (i + 1) * blk
    qt = qt_ref[...]
    ch = lax.broadcasted_iota(jnp.int32, (pair_w, 1), 0)
    key_r = lax.broadcasted_iota(jnp.int32, (blk, blk), 0)
    qry_c = lax.broadcasted_iota(jnp.int32, (blk, blk), 1)
    causal = key_r <= qry_c
    scale = HEAD_DIM ** -0.5 * LOG2_E

    qhs = [jnp.where((ch >= h * HEAD_DIM) & (ch < (h + 1) * HEAD_DIM), qt, 0.0) for h in range(2)]
    qs = jnp.concatenate([(qh * scale).astype(BF16) for qh in qhs], axis=1)
    st = jnp.dot(k_ref[0:nk, :], qs, preferred_element_type=F32)
    causal2 = jnp.concatenate([causal, causal], axis=1)
    pieces = []
    if i > MOBA_TOPK:
        sels = [_topk_rows(_dot_split(means_ref[...], qh, _NN), i, MOBA_TOPK) for qh in qhs]
        for j in range(i):
            sel_j = jnp.concatenate([jnp.where(s[j], 1.0, 0.0) for s in sels], axis=1)
            pieces.append(jnp.where(sel_j > 0.5, st[j * blk:(j + 1) * blk], NEG_INF))
    else:
        pieces = [st[j * blk:(j + 1) * blk] for j in range(i)]
    pieces.append(jnp.where(causal2, st[i * blk:], NEG_INF))
    sm = jnp.concatenate(pieces, axis=0) if i else pieces[0]
    m = jnp.max(sm, axis=0, keepdims=True)
    p = jnp.exp2(sm - m).astype(BF16)
    vt1 = jnp.concatenate([vt_ref[:, 0:nk], jnp.ones((BF16_SUBLANES, nk), BF16)], axis=0)
    acc = jnp.dot(vt1, p, preferred_element_type=F32)
    out2 = acc[0:pair_w] / acc[pair_w:pair_w + 1]
    out_t = jnp.where(ch < HEAD_DIM, out2[:, 0:blk], out2[:, blk:])
    o_ref[...] = out_t.T.astype(o_ref.dtype)


def _moba_prompt_kernel(qt_ref, k_ref, vt_ref, means_ref, o_ref, *, nblk):
    i = pl.program_id(2)
    for c in range(nblk):
        pl.when(i == c)(functools.partial(_moba_prompt_block, c, qt_ref, k_ref, vt_ref, means_ref, o_ref))


def _moba_prompt(qt, kbf, vtb, means, n_batch, seq):
    assert seq % MOBA_BLOCK == 0
    nblk = seq // MOBA_BLOCK
    pair_w = 2 * HEAD_DIM
    n_pairs = BRANCH_W // pair_w
    out = pl.pallas_call(
        functools.partial(_moba_prompt_kernel, nblk=nblk),
        grid=(n_batch, n_pairs, nblk),
        in_specs=[pl.BlockSpec((None, pair_w, MOBA_BLOCK), lambda b, hp, i: (b, hp, i)),
                  pl.BlockSpec((None, seq, pair_w), lambda b, hp, i: (b, 0, hp)),
                  pl.BlockSpec((None, pair_w, seq), lambda b, hp, i: (b, hp, 0)),
                  pl.BlockSpec((None, nblk, pair_w), lambda b, hp, i: (b, 0, hp))],
        out_specs=pl.BlockSpec((None, MOBA_BLOCK, pair_w), lambda b, hp, i: (b, i, hp)),
        out_shape=jax.ShapeDtypeStruct((n_batch, seq, BRANCH_W), BF16),
        compiler_params=_cparams(3),
        name="moba_prompt",
    )(qt, kbf.reshape(n_batch, seq, BRANCH_W), vtb, means)
    return out.reshape(n_batch * seq, BRANCH_W)


CHUNK_PAGES = 8
DMA_SLOTS = 4


def _moba_sample_kernel(pt_ref, q_ref, kn_ref, vn_ref, ck_ref, cv_ref, o_ref,
                        buf, sem, s_sc, *, layer, n_batch, n_pages, sq):
    b = pl.program_id(0)
    ppb = MOBA_BLOCK // PAGE_SIZE
    nblk = n_pages // ppb
    n_chunks = n_pages // CHUNK_PAGES
    bpc = CHUNK_PAGES // ppb
    nrow = N_HEADS * sq

    def chunk_copies(bb, c, slot):
        src = ck_ref if c < n_chunks else cv_ref
        cc = c % n_chunks
        return [pltpu.make_async_copy(src.at[layer, pt_ref[bb, cc * CHUNK_PAGES + p]],
                                      buf.at[slot, p], sem.at[slot])
                for p in range(CHUNK_PAGES)]

    def start_chunk(bb, c):
        for cp in chunk_copies(bb, c, c % DMA_SLOTS):
            cp.start()

    def wait_chunk(bb, c):
        for cp in chunk_copies(bb, c, c % DMA_SLOTS):
            cp.wait()

    def prefetch(c):
        nxt = c + DMA_SLOTS - 1
        if nxt < 2 * n_chunks:
            start_chunk(b, nxt)
        else:
            @pl.when(b + 1 < n_batch)
            def _():
                start_chunk(b + 1, nxt - 2 * n_chunks)

    @pl.when(b == 0)
    def _():
        for c in range(DMA_SLOTS - 1):
            start_chunk(b, c)

    qb = q_ref[...]
    qrows = jnp.concatenate([qb] * N_HEADS, axis=0)
    rh = lax.broadcasted_iota(jnp.int32, (nrow, BRANCH_W), 0) >> (sq.bit_length() - 1)
    chh = lax.broadcasted_iota(jnp.int32, (nrow, BRANCH_W), 1) >> (HEAD_DIM.bit_length() - 1)
    head_mask = rh == chh
    qrows = jnp.where(head_mask, qrows, 0.0)
    qs = (qrows * (HEAD_DIM ** -0.5)).astype(BF16)

    def block_of(slot, n):
        return jnp.concatenate([buf[slot, pg] for pg in range(n * ppb, (n + 1) * ppb)], axis=1)

    lane_i = lax.broadcasted_iota(jnp.int32, (nrow, LANES), 1)
    gs = jnp.zeros((nrow, LANES), F32)
    for c in range(n_chunks):
        prefetch(c)
        wait_chunk(b, c)
        for n in range(bpc):
            blk_id = c * bpc + n
            s = jnp.dot(qs, block_of(c % DMA_SLOTS, n).astype(BF16), preferred_element_type=F32)
            s_sc[blk_id] = s
            gs = jnp.where(lane_i == blk_id, jnp.sum(s, axis=1, keepdims=True), gs)

    lane = lane_i.astype(F32)
    sel_f = jnp.zeros(gs.shape, F32)
    work = jnp.where(lane_i < nblk, gs, -jnp.inf)
    for _ in range(min(MOBA_TOPK, nblk)):
        mx = jnp.max(work, axis=1, keepdims=True)
        first = jnp.min(jnp.where(work == mx, lane, float(nblk)), axis=1, keepdims=True)
        pick = lane == first
        sel_f = jnp.where(pick, 1.0, sel_f)
        work = jnp.where(pick, -jnp.inf, work)

    s_own = lax.dot_general(qs, kn_ref[...].astype(BF16), _NT, preferred_element_type=F32)
    qi = lax.broadcasted_iota(jnp.int32, (nrow, sq), 0) & (sq - 1)
    kk = lax.broadcasted_iota(jnp.int32, (nrow, sq), 1)
    s_own = jnp.where(kk <= qi, s_own, NEG_INF)
    def masked(blk_id):
        return jnp.where(sel_f[:, blk_id:blk_id + 1] > 0.5, s_sc[blk_id], NEG_INF)

    m_run = masked(0)
    for n in range(1, nblk):
        m_run = jnp.maximum(m_run, masked(n))
    m = jnp.maximum(jnp.max(s_own, axis=1, keepdims=True), jnp.max(m_run, axis=1, keepdims=True))
    p_own = jnp.exp(s_own - m)
    acc = jnp.dot(p_own.astype(BF16), vn_ref[...].astype(BF16), preferred_element_type=F32)

    l_run = jnp.zeros((nrow, MOBA_BLOCK), F32)
    for c in range(n_chunks, 2 * n_chunks):
        prefetch(c)
        wait_chunk(b, c)
        for n in range(bpc):
            p = jnp.exp(masked((c - n_chunks) * bpc + n) - m)
            l_run = l_run + p
            acc = acc + lax.dot_general(p.astype(BF16), block_of(c % DMA_SLOTS, n).astype(BF16), _NT,
                                        preferred_element_type=F32)
    l = jnp.sum(p_own, axis=1, keepdims=True) + jnp.sum(l_run, axis=1, keepdims=True)

    full = jnp.where(head_mask, acc / l, 0.0)
    out = full[0:sq]
    for h in range(1, N_HEADS):
        out = out + full[h * sq:(h + 1) * sq]
    o_ref[...] = out.astype(o_ref.dtype)


def _moba_sample(q, k_new, v_new, cache_k, cache_v, page_table, layer, n_batch, sq):
    n_pages = page_table.shape[1]
    assert (n_pages * PAGE_SIZE) % MOBA_BLOCK == 0, "own block must hold only the new rows"
    assert n_pages % CHUNK_PAGES == 0 and (2 * n_pages // CHUNK_PAGES) % DMA_SLOTS == 0
    nblk = n_pages * PAGE_SIZE // MOBA_BLOCK
    assert nblk <= LANES, "block scores are kept one lane per block"
    depth, n_pool = cache_k.shape[0], cache_k.shape[1]
    ck = cache_k.transpose(0, 1, 3, 4, 2).reshape(depth, n_pool, BRANCH_W, PAGE_SIZE)
    cv = cache_v.transpose(0, 1, 3, 4, 2).reshape(depth, n_pool, BRANCH_W, PAGE_SIZE)
    q3, k3, v3 = (t.reshape(n_batch, sq, BRANCH_W) for t in (q, k_new, v_new))
    row = pl.BlockSpec((None, sq, BRANCH_W), lambda b, pt: (b, 0, 0))
    hbm = pl.BlockSpec(memory_space=pl.ANY)
    out = pl.pallas_call(
        functools.partial(_moba_sample_kernel, layer=layer, n_batch=n_batch, n_pages=n_pages, sq=sq),
        grid_spec=pltpu.PrefetchScalarGridSpec(
            num_scalar_prefetch=1,
            grid=(n_batch,),
            in_specs=[row, row, row, hbm, hbm],
            out_specs=row,
            scratch_shapes=[pltpu.VMEM((DMA_SLOTS, CHUNK_PAGES, BRANCH_W, PAGE_SIZE), F32),
                            pltpu.SemaphoreType.DMA((DMA_SLOTS,)),
                            pltpu.VMEM((nblk, N_HEADS * sq, MOBA_BLOCK), F32)]),
        out_shape=jax.ShapeDtypeStruct((n_batch, sq, BRANCH_W), BF16),
        compiler_params=_cparams(1),
        name="moba_sample",
    )(page_table, q3, k3, v3, ck, cv)
    return out.reshape(n_batch * sq, BRANCH_W)


def _merge_kernel(x_ref, a_ref, b_ref, c_ref, gpre_ref, gpost_ref, wg_ref, proj_ref, wo_ref, o_ref):
    x = x_ref[...]
    u = _rms(x, gpre_ref[...]).astype(BF16)
    gates = jax.nn.sigmoid(jnp.dot(u, wg_ref[...], preferred_element_type=F32))
    merged = None
    for n, br in enumerate((a_ref, b_ref, c_ref)):
        t = gates[:, n * D_MODEL:(n + 1) * D_MODEL] * jnp.dot(br[...], proj_ref[n], preferred_element_type=F32)
        merged = t if merged is None else merged + t
    m = jnp.dot(merged.astype(BF16), wo_ref[...], preferred_element_type=F32)
    o_ref[...] = x + _rms(m, gpost_ref[...])


def _merge(x, a, b, c, gpre, gpost, wg, proj, wo, tm):
    n = x.shape[0]
    row = lambda w: pl.BlockSpec((tm, w), lambda i: (i, 0))
    consts = (gpre, gpost, wg, proj, wo)
    return pl.pallas_call(
        _merge_kernel,
        grid=(n // tm,),
        in_specs=[row(D_MODEL)] + [row(BRANCH_W)] * 3 + [_const_spec(t.shape) for t in consts],
        out_specs=row(D_MODEL),
        out_shape=jax.ShapeDtypeStruct((n, D_MODEL), F32),
        compiler_params=_cparams(1),
        name="merge",
    )(x, a, b, c, *consts)


def _ffn_dense_kernel(x_ref, gpre_ref, gpost_ref, wg_ref, wu_ref, wd_ref, o_ref, u_sc, acc_sc):
    f = pl.program_id(1)

    @pl.when(f == 0)
    def _():
        u_sc[...] = _rms(x_ref[...], gpre_ref[...]).astype(BF16)
        acc_sc[...] = jnp.zeros(acc_sc.shape, F32)

    u = u_sc[...]
    hg = jnp.dot(u, wg_ref[...], preferred_element_type=F32)
    hu = jnp.dot(u, wu_ref[...], preferred_element_type=F32)
    acc_sc[...] += jnp.dot((jax.nn.silu(hg) * hu).astype(BF16), wd_ref[...], preferred_element_type=F32)

    @pl.when(f == pl.num_programs(1) - 1)
    def _():
        o_ref[...] = x_ref[...] + _rms(acc_sc[...], gpost_ref[...])


def _ffn_dense(x, gpre, gpost, wg, wu, wd, tm, fc):
    n = x.shape[0]
    dff = wg.shape[1]
    row = pl.BlockSpec((tm, D_MODEL), lambda i, f: (i, 0))
    return pl.pallas_call(
        _ffn_dense_kernel,
        grid=(n // tm, dff // fc),
        in_specs=[row, _const_spec(gpre.shape), _const_spec(gpost.shape),
                  pl.BlockSpec((D_MODEL, fc), lambda i, f: (0, f)),
                  pl.BlockSpec((D_MODEL, fc), lambda i, f: (0, f)),
                  pl.BlockSpec((fc, D_MODEL), lambda i, f: (f, 0))],
        out_specs=row,
        out_shape=jax.ShapeDtypeStruct((n, D_MODEL), F32),
        scratch_shapes=[pltpu.VMEM((tm, D_MODEL), BF16), pltpu.VMEM((tm, D_MODEL), F32)],
        compiler_params=_cparams(2),
        name="ffn_dense",
    )(x, gpre, gpost, wg, wu, wd)


def _router_gate(u, router_ref):
    logits = _dot_split(u, router_ref[...], _NN)
    lane = lax.broadcasted_iota(jnp.int32, logits.shape, 1)
    logits = jnp.where(lane < N_EXPERTS, logits, -jnp.inf)
    m1 = jnp.max(logits, axis=1, keepdims=True)
    i1 = jnp.min(jnp.where(logits == m1, lane, LANES), axis=1, keepdims=True)
    rest = jnp.where(lane == i1, -jnp.inf, logits)
    m2 = jnp.max(rest, axis=1, keepdims=True)
    i2 = jnp.min(jnp.where(rest == m2, lane, LANES), axis=1, keepdims=True)
    e2 = jnp.exp(m2 - m1)
    w1 = 1.0 / (1.0 + e2)
    w2 = e2 / (1.0 + e2)
    gate = jnp.where(lane == i1, w1, 0.0) + jnp.where(lane == i2, w2, 0.0)
    chosen = jnp.where((lane == i1) | (lane == i2), 1.0, 0.0)
    return gate, chosen


MOE_CHUNK = 128
MOE_SLAB = 256


def _moe_kernel(x_ref, gpre_ref, gpost_ref, router_ref, wg_ref, wu_ref, wd_ref, o_ref,
                u_sc, gate_sc, rank_sc, rankt_sc, cnt_sc, xs_sc, y_sc, *, tm):
    e = pl.program_id(1)
    f = pl.program_id(2)
    nf = pl.num_programs(2)
    ck = MOE_CHUNK

    @pl.when((e == 0) & (f == 0))
    def _():
        u = _rms(x_ref[...], gpre_ref[...])
        u_sc[...] = u.astype(BF16)
        gate, chosen = _router_gate(u, router_ref)
        gate_sc[...] = gate
        earlier = (lax.broadcasted_iota(jnp.int32, (tm, tm), 1) < lax.broadcasted_iota(jnp.int32, (tm, tm), 0))
        rank = jnp.dot(jnp.where(earlier, 1.0, 0.0).astype(BF16), chosen.astype(BF16), preferred_element_type=F32)
        rank = jnp.where(chosen > 0.5, rank, -1.0)
        rank_sc[...] = rank
        rankt_sc[...] = rank.T[0:SUBLANES, :]
        cnt_sc[...] = jnp.sum(chosen, axis=0, keepdims=True)
        o_ref[...] = jnp.zeros(o_ref.shape, F32)
        y_sc[...] = jnp.zeros(y_sc.shape, F32)

    lane1 = lax.broadcasted_iota(jnp.int32, (1, LANES), 1)
    cnt = jnp.sum(jnp.where(lane1 == e, cnt_sc[...], 0.0))
    n_chunks = jnp.ceil(cnt * (1.0 / ck)).astype(jnp.int32)

    @pl.when(f == 0)
    def _():
        rank_row = rankt_sc[pl.ds(e, 1), :]
        slot = lax.broadcasted_iota(jnp.int32, (ck, 1), 0).astype(F32)

        def compact(c, carry):
            base = pl.multiple_of(c * ck, ck)
            onehot = jnp.where(rank_row - (c * ck).astype(F32) == slot, 1.0, 0.0).astype(BF16)
            xs_sc[pl.ds(base, ck), :] = jnp.dot(onehot, u_sc[...], preferred_element_type=F32).astype(BF16)
            return carry

        lax.fori_loop(0, n_chunks, compact, 0)

    def expert(c, carry):
        base = pl.multiple_of(c * ck, ck)
        xs = xs_sc[pl.ds(base, ck), :]
        hg = jnp.dot(xs, wg_ref[...], preferred_element_type=F32)
        hu = jnp.dot(xs, wu_ref[...], preferred_element_type=F32)
        y = jnp.dot((jax.nn.silu(hg) * hu).astype(BF16), wd_ref[...], preferred_element_type=F32)

        @pl.when(f == 0)
        def _():
            y_sc[pl.ds(base, ck), :] = y

        @pl.when(f > 0)
        def _():
            y_sc[pl.ds(base, ck), :] += y

        return carry

    lax.fori_loop(0, n_chunks, expert, 0)

    @pl.when(f == nf - 1)
    def _():
        ck2 = 2 * ck
        lane = lax.broadcasted_iota(jnp.int32, (MOE_SLAB, LANES), 1)
        slot = lax.broadcasted_iota(jnp.int32, (1, ck2), 1).astype(F32)

        def scatter(c, carry):
            base = pl.multiple_of(c * ck2, ck2)
            yb = y_sc[pl.ds(base, ck2), :].astype(BF16)
            for s0 in range(0, tm, MOE_SLAB):
                rows = slice(s0, s0 + MOE_SLAB)
                rank_col = jnp.sum(jnp.where(lane == e, rank_sc[rows, :], 0.0), axis=1, keepdims=True)
                gate_col = jnp.sum(jnp.where(lane == e, gate_sc[rows, :], 0.0), axis=1, keepdims=True)
                onehot_t = jnp.where(rank_col - (c * ck2).astype(F32) == slot, 1.0, 0.0).astype(BF16)
                o_ref[rows, :] += gate_col * jnp.dot(onehot_t, yb, preferred_element_type=F32)
            return carry

        lax.fori_loop(0, (n_chunks + 1) // 2, scatter, 0)

    @pl.when((e == pl.num_programs(1) - 1) & (f == nf - 1))
    def _():
        o_ref[...] = x_ref[...] + _rms(o_ref[...], gpost_ref[...])


def _moe(x, gpre, gpost, router_pad, wg, wu, wd, tm, fc):
    n = x.shape[0]
    n_exp, _, dff = wg.shape
    assert n_exp <= SUBLANES and tm % MOE_SLAB == 0 and MOE_SLAB % MOE_CHUNK == 0
    cap = pl.cdiv(tm, 2 * MOE_CHUNK) * 2 * MOE_CHUNK
    row = pl.BlockSpec((tm, D_MODEL), lambda i, e, f: (i, 0))
    return pl.pallas_call(
        functools.partial(_moe_kernel, tm=tm),
        grid=(n // tm, n_exp, dff // fc),
        in_specs=[row, _const_spec(gpre.shape), _const_spec(gpost.shape), _const_spec(router_pad.shape),
                  pl.BlockSpec((None, D_MODEL, fc), lambda i, e, f: (e, 0, f)),
                  pl.BlockSpec((None, D_MODEL, fc), lambda i, e, f: (e, 0, f)),
                  pl.BlockSpec((None, fc, D_MODEL), lambda i, e, f: (e, f, 0))],
        out_specs=row,
        out_shape=jax.ShapeDtypeStruct((n, D_MODEL), F32),
        scratch_shapes=[pltpu.VMEM((tm, D_MODEL), BF16), pltpu.VMEM((tm, LANES), F32),
                        pltpu.VMEM((tm, LANES), F32), pltpu.VMEM((SUBLANES, tm), F32),
                        pltpu.VMEM((1, LANES), F32),
                        pltpu.VMEM((cap, D_MODEL), BF16), pltpu.VMEM((cap, D_MODEL), F32)],
        compiler_params=_cparams(3),
        name="moe",
    )(x, gpre, gpost, router_pad, wg, wu, wd)


def _block_diag_halves(w_a, w_i):
    half_blocks = LRU_BLOCKS // 2

    def bd(w4):
        eye = jnp.eye(half_blocks, dtype=w4.dtype)
        return jnp.einsum('hcd,hg->hcgd', w4, eye).reshape(half_blocks * LRU_BLOCK_W, half_blocks * LRU_BLOCK_W)

    halves = [jnp.concatenate([bd(w_a[i * half_blocks:(i + 1) * half_blocks]),
                               bd(w_i[i * half_blocks:(i + 1) * half_blocks])], axis=1) for i in range(2)]
    return jnp.stack(halves).astype(BF16)


def _row(v):
    return v.reshape(1, -1)


def _pad_hist(h, pad_rows):
    return jnp.pad(h, ((0, 0), (pad_rows - h.shape[1], 0), (0, 0)))


PROMPT_TS = 512
ROW_TM = 512
FFN_FC = 1408
MOE_FC = 1792
MOE_TM = 1024


def kernel(x_prompt, x_sample, cache_k, cache_v, page_table, state_pool, state_conv, state_h, norm_mix_pre, norm_mix_post, norm_ffn_pre, norm_ffn_post, w_in, pool_w, pool_scale, conv_w, conv_b, lru_wa, lru_ba, lru_wi, lru_bi, lru_lambda, proj_pool, proj_attn, proj_lru, w_o, ffn_w_gate, ffn_w_up, ffn_w_down, router_w, moe_w_gate, moe_w_up, moe_w_down):
    depth = w_in.shape[0]
    bp, sp, _ = x_prompt.shape
    bs, ss, _ = x_sample.shape
    past_len = page_table.shape[1] * PAGE_SIZE
    xp = x_prompt.reshape(bp * sp, D_MODEL)
    xs = x_sample.reshape(bs * ss, D_MODEL)
    tm_s = bs * ss

    outs = {name: [] for name in ("ks", "vs", "pp", "ps", "cp", "cs", "hp", "hs")}
    kv_pages = tuple(jnp.zeros((depth, bp, sp // PAGE_SIZE, BRANCH_W, PAGE_SIZE), F32) for _ in range(2))
    for l in range(depth):
        w_qkv = w_in[l, :, :D_QKV].astype(BF16)
        w_gates = w_in[l, :, D_QKV:].astype(BF16)
        mix_consts = (_row(norm_mix_pre[l]), w_qkv, pool_w[l].astype(BF16), _row(pool_scale[l]),
                      conv_w[l], _row(conv_b[l]), _block_diag_halves(lru_wa[l], lru_wi[l]),
                      _row(lru_ba[l]), _row(lru_bi[l]), _row(lru_lambda[l]))
        proj = jnp.stack([proj_pool[l], proj_attn[l], proj_lru[l]]).astype(BF16)
        merge_consts = (_row(norm_mix_pre[l]), _row(norm_mix_post[l]), w_gates, proj, w_o[l].astype(BF16))

        qt_p, kbf_p, vtb_p, kmean_p, kpages, vpages, a_p, c_p, pt_p, ct_p, h_p = _mixer_in_prompt(
            xp, bp, sp, PROMPT_TS, l, depth, kv_pages, *mix_consts)
        kv_pages = (kpages, vpages)
        b_p = _moba_prompt(qt_p, kbf_p, vtb_p, kmean_p.reshape(bp, sp // MOBA_BLOCK, BRANCH_W), bp, sp)
        xp = _merge(xp, a_p, b_p, c_p, *merge_consts, ROW_TM)

        q_s, k_s, v_s, a_s, c_s, pt_s, ct_s, h_s = _mixer_in_sample(
            xs, bs, ss, past_len, _pad_hist(state_pool[l], POOL_PAD), _pad_hist(state_conv[l], CONV_PAD),
            state_h[l].reshape(bs, 1, BRANCH_W), *mix_consts)
        b_s = _moba_sample(q_s, k_s, v_s, cache_k, cache_v, page_table, l, bs, ss)
        xs = _merge(xs, a_s, b_s, c_s, *merge_consts, tm_s)

        gpre, gpost = _row(norm_ffn_pre[l]), _row(norm_ffn_post[l])
        j = l // 2
        if l % 2 == 0:
            wg, wu, wd = (w[j].astype(BF16) for w in (ffn_w_gate, ffn_w_up, ffn_w_down))
            xp = _ffn_dense(xp, gpre, gpost, wg, wu, wd, ROW_TM, FFN_FC)
            xs = _ffn_dense(xs, gpre, gpost, wg, wu, wd, tm_s, FFN_FC)
        else:
            wg, wu, wd = (w[j].astype(BF16) for w in (moe_w_gate, moe_w_up, moe_w_down))
            router_pad = jnp.pad(router_w[j], ((0, 0), (0, LANES - N_EXPERTS)))
            xp = _moe(xp, gpre, gpost, router_pad, wg, wu, wd, MOE_TM, MOE_FC)
            xs = _moe(xs, gpre, gpost, router_pad, wg, wu, wd, tm_s, MOE_FC)

        outs["ks"].append(k_s.reshape(bs, ss, N_HEADS, HEAD_DIM))
        outs["vs"].append(v_s.reshape(bs, ss, N_HEADS, HEAD_DIM))
        outs["pp"].append(pt_p[:, POOL_PAD - POOL_HIST:])
        outs["ps"].append(pt_s[:, POOL_PAD - POOL_HIST:])
        outs["cp"].append(ct_p[:, CONV_PAD - (CONV_W - 1):])
        outs["cs"].append(ct_s[:, CONV_PAD - (CONV_W - 1):])
        outs["hp"].append(h_p.reshape(bp, BRANCH_W))
        outs["hs"].append(h_s.reshape(bs, BRANCH_W))

    st = {k: jnp.stack(v) for k, v in outs.items()}
    kp, vp = (t.reshape(depth, bp, sp // PAGE_SIZE, N_HEADS, HEAD_DIM, PAGE_SIZE).transpose(0, 1, 2, 5, 3, 4)
              for t in kv_pages)
    return (xp.reshape(bp, sp, D_MODEL), xs.reshape(bs, ss, D_MODEL),
            kp, vp, st["ks"], st["vs"], st["pp"], st["ps"], st["cp"], st["cs"], st["hp"], st["hs"])
```

```python
import functools

import jax
import jax.numpy as jnp
from jax import lax
from jax.experimental import pallas as pl
from jax.experimental.pallas import tpu as pltpu

F32 = jnp.float32
BF16 = jnp.bfloat16

D_MODEL = 1024
BRANCH_W = 512
POOL_WINDOWS = (2, 4, 8, 16)
POOL_GROUP_W = BRANCH_W // len(POOL_WINDOWS)
POOL_HIST = max(POOL_WINDOWS) - 1
POOL_PAD = 16
N_HEADS = 8
HEAD_DIM = BRANCH_W // N_HEADS
MOBA_BLOCK = 256
MOBA_TOPK = 3
PAGE_SIZE = 128
LRU_BLOCKS = 8
LRU_BLOCK_W = BRANCH_W // LRU_BLOCKS
CONV_W = 4
CONV_PAD = 8
LRU_C = 8.0
N_EXPERTS = 8
TOP_K = 2
EPS = 1e-6
NEG_INF = -1e30
LOG2_E = 1.4426950408889634
D_QKV = 6 * BRANCH_W
SUBLANES = 8
BF16_SUBLANES = 16
LANES = 128
VMEM_LIMIT = 56 * 1024 * 1024


def _cparams(n_axes):
    return pltpu.CompilerParams(
        dimension_semantics=("arbitrary",) * n_axes, vmem_limit_bytes=VMEM_LIMIT)


def _rms(x, g):
    return x * lax.rsqrt(jnp.mean(x * x, axis=-1, keepdims=True) + EPS) * g


def _split_bf16(x):
    hi = x.astype(BF16)
    lo = (x - hi.astype(F32)).astype(BF16)
    return hi, lo


def _dot_split(a, b, dims):
    a_hi, a_lo = _split_bf16(a)
    b_hi, b_lo = _split_bf16(b)
    d = functools.partial(lax.dot_general, dimension_numbers=dims, preferred_element_type=F32)
    return d(a_hi, b_hi) + (d(a_hi, b_lo) + d(a_lo, b_hi))


_NN = (((1,), (0,)), ((), ()))
_NT = (((1,), (1,)), ((), ()))


def _sigmoid(x):
    return 0.5 * jnp.tanh(0.5 * x) + 0.5


def _softplus(x):
    return jnp.maximum(x, 0.0) + jnp.log1p(jnp.exp(-jnp.abs(x)))


def _lru_coeffs(xc, wbd_ref, ba, bi, lam, first_row_is_start):
    xcb = xc.astype(BF16)
    half = BRANCH_W // 2
    g0 = jnp.dot(xcb[:, :half], wbd_ref[0], preferred_element_type=F32)
    g1 = jnp.dot(xcb[:, half:], wbd_ref[1], preferred_element_type=F32)
    pre_a = jnp.concatenate([g0[:, :half], g1[:, :half]], axis=1)
    pre_i = jnp.concatenate([g0[:, half:], g1[:, half:]], axis=1)
    r = _sigmoid(pre_a + ba)
    i = _sigmoid(pre_i + bi)
    log_a = (-LRU_C) * r * _softplus(-lam)
    a = jnp.exp(log_a)
    t = jnp.tanh(log_a)
    mult = jnp.sqrt(-2.0 * t) * lax.rsqrt(1.0 - t)
    if first_row_is_start is not None:
        mult = jnp.where(first_row_is_start, 1.0, mult)
    return a, mult * i * xc


def _scan8(a, u):
    rows, width = a.shape
    a = a.reshape(rows // SUBLANES, SUBLANES, width)
    u = u.reshape(rows // SUBLANES, SUBLANES, width)
    r8 = lax.broadcasted_iota(jnp.int32, a.shape, 1)
    for d in (1, 2, 4):
        a_s = pltpu.roll(a, d, axis=1)
        u_s = pltpu.roll(u, d, axis=1)
        m = r8 >= d
        u = jnp.where(m, u + a * u_s, u)
        a = jnp.where(m, a * a_s, a)
    return a.reshape(rows, width), u.reshape(rows, width)


def _pool_mix(win_groups, xp, cnt_groups, poolw_ref, pscale):
    outs = []
    for g in range(len(POOL_WINDOWS)):
        c0 = g * POOL_GROUP_W
        pooled = win_groups[g] / cnt_groups[g] - xp[:, c0:c0 + POOL_GROUP_W]
        outs.append(jnp.dot(pooled.astype(BF16), poolw_ref[g], preferred_element_type=F32))
    return jnp.concatenate(outs, axis=1) * pscale


def _mixer_in_prompt_kernel(x_ref, gpre_ref, win_ref, poolw_ref, pscale_ref, convw_ref, convb_ref,
                            wbd_ref, ba_ref, bi_ref, lam_ref, *rest, ts):
    (qt_ref, kbf_ref, vtb_ref, kmean_ref, kpage_ref, vpage_ref, a_ref, c_ref,
     ptail_ref, ctail_ref, hlast_ref, pbuf, cbuf, hcar) = rest[-14:]
    s = pl.program_id(1)

    @pl.when(s == 0)
    def _():
        pbuf[0:POOL_PAD, :] = jnp.zeros((POOL_PAD, BRANCH_W), F32)
        cbuf[0:CONV_PAD, :] = jnp.zeros((CONV_PAD, BRANCH_W), F32)
        hcar[...] = jnp.zeros((1, BRANCH_W), F32)

    u = _rms(x_ref[...], gpre_ref[...])
    z = jnp.dot(u.astype(BF16), win_ref[...], preferred_element_type=F32)
    bw = BRANCH_W
    xp = z[:, 0:bw]
    k = z[:, 2 * bw:3 * bw]
    xl = z[:, 4 * bw:5 * bw]
    gl = z[:, 5 * bw:6 * bw]

    qt_ref[...] = z[:, bw:2 * bw].T
    kbf_ref[...] = k.astype(BF16)
    kt = k.T
    vt = z[:, 3 * bw:4 * bw].T
    vtb_ref[...] = vt.astype(BF16)
    for n in range(ts // MOBA_BLOCK):
        kmean_ref[n:n + 1, :] = jnp.sum(k[n * MOBA_BLOCK:(n + 1) * MOBA_BLOCK], axis=0, keepdims=True) * (1.0 / MOBA_BLOCK)
    for p in range(ts // PAGE_SIZE):
        kpage_ref[p] = kt[:, p * PAGE_SIZE:(p + 1) * PAGE_SIZE]
        vpage_ref[p] = vt[:, p * PAGE_SIZE:(p + 1) * PAGE_SIZE]

    pos = s * ts + lax.broadcasted_iota(jnp.int32, (ts, 1), 0)

    pbuf[POOL_PAD:POOL_PAD + ts, :] = xp
    wins, cnts = [], []
    for g, w in enumerate(POOL_WINDOWS):
        c0 = g * POOL_GROUP_W
        win = pbuf[:, c0:c0 + POOL_GROUP_W]
        d = 1
        while d < w:
            win = win + pltpu.roll(win, d, axis=0)
            d *= 2
        wins.append(win[POOL_PAD:])
        cnts.append(jnp.minimum(pos + 1, w).astype(F32))
    a_ref[...] = _pool_mix(wins, xp, cnts, poolw_ref, pscale_ref[...]).astype(a_ref.dtype)
    tail = pbuf[ts:ts + POOL_PAD, :]
    pbuf[0:POOL_PAD, :] = tail
    ptail_ref[...] = tail

    cbuf[CONV_PAD:CONV_PAD + ts, :] = xl
    xc = convb_ref[...] + xl * convw_ref[CONV_W - 1:CONV_W, :]
    for j in range(CONV_W - 1):
        off = CONV_PAD - (CONV_W - 1) + j
        xc = xc + cbuf[off:off + ts, :] * convw_ref[j:j + 1, :]
    ctail = cbuf[ts:ts + CONV_PAD, :]
    cbuf[0:CONV_PAD, :] = ctail
    ctail_ref[...] = ctail

    a, uu = _lru_coeffs(xc, wbd_ref, ba_ref[...], bi_ref[...], lam_ref[...], pos == 0)
    a, uu = _scan8(a, uu)
    carry = jnp.broadcast_to(hcar[...], (SUBLANES, BRANCH_W))
    hs = []
    for g in range(ts // SUBLANES):
        hg = a[g * SUBLANES:(g + 1) * SUBLANES] * carry + uu[g * SUBLANES:(g + 1) * SUBLANES]
        hs.append(hg)
        carry = jnp.broadcast_to(hg[SUBLANES - 1:SUBLANES, :], (SUBLANES, BRANCH_W))
    h = jnp.concatenate(hs, axis=0)
    hcar[...] = carry[0:1, :]
    hlast_ref[...] = carry[0:1, :]
    c_ref[...] = (h * jax.nn.gelu(gl)).astype(c_ref.dtype)


def _const_spec(shape):
    nd = len(shape)
    return pl.BlockSpec(shape, lambda *_: (0,) * nd)


def _mixer_in_prompt(x, n_batch, seq, ts, layer, depth, kv_pages, gpre, win, poolw, pscale, convw, convb,
                     wbd, ba, bi, lam):
    assert ts % MOBA_BLOCK == 0 and seq % ts == 0
    n = n_batch * seq
    nt = seq // ts
    ppt = ts // PAGE_SIZE
    bpt = ts // MOBA_BLOCK
    row = lambda w: pl.BlockSpec((ts, w), lambda b, s: (b * nt + s, 0))
    chan = pl.BlockSpec((None, BRANCH_W, ts), lambda b, s: (b, 0, s))
    per_b = lambda r: pl.BlockSpec((None, r, BRANCH_W), lambda b, s: (b, 0, 0))
    pages = pl.BlockSpec((None, None, ppt, BRANCH_W, PAGE_SIZE), lambda b, s: (layer, b, s, 0, 0))
    page_shape = jax.ShapeDtypeStruct((depth, n_batch, seq // PAGE_SIZE, BRANCH_W, PAGE_SIZE), F32)
    consts = (gpre, win, poolw, pscale, convw, convb, wbd, ba, bi, lam)
    extra = tuple(kv_pages)
    assert all(t.shape == page_shape.shape and t.dtype == page_shape.dtype for t in extra)
    n_in = 1 + len(consts)
    return pl.pallas_call(
        functools.partial(_mixer_in_prompt_kernel, ts=ts),
        grid=(n_batch, nt),
        in_specs=[row(D_MODEL)] + [_const_spec(c.shape) for c in consts]
        + [pl.BlockSpec(memory_space=pl.ANY)] * len(extra),
        out_specs=[chan, row(BRANCH_W), chan,
                   pl.BlockSpec((None, None, bpt, BRANCH_W), lambda b, s: (b, s, 0, 0)),
                   pages, pages, row(BRANCH_W), row(BRANCH_W),
                   per_b(POOL_PAD), per_b(CONV_PAD), per_b(1)],
        out_shape=[jax.ShapeDtypeStruct((n_batch, BRANCH_W, seq), F32),
                   jax.ShapeDtypeStruct((n, BRANCH_W), BF16),
                   jax.ShapeDtypeStruct((n_batch, BRANCH_W, seq), BF16),
                   jax.ShapeDtypeStruct((n_batch, nt, bpt, BRANCH_W), F32),
                   page_shape, page_shape,
                   jax.ShapeDtypeStruct((n, BRANCH_W), BF16),
                   jax.ShapeDtypeStruct((n, BRANCH_W), BF16),
                   jax.ShapeDtypeStruct((n_batch, POOL_PAD, BRANCH_W), F32),
                   jax.ShapeDtypeStruct((n_batch, CONV_PAD, BRANCH_W), F32),
                   jax.ShapeDtypeStruct((n_batch, 1, BRANCH_W), F32)],
        scratch_shapes=[pltpu.VMEM((POOL_PAD + ts, BRANCH_W), F32),
                        pltpu.VMEM((CONV_PAD + ts, BRANCH_W), F32),
                        pltpu.VMEM((1, BRANCH_W), F32)],
        input_output_aliases={n_in + i: 4 + i for i in range(len(extra))},
        compiler_params=_cparams(2),
        name="mixer_in_prompt",
    )(x, *consts, *extra)


def _mixer_in_sample_kernel(x_ref, phist_ref, chist_ref, h0_ref, gpre_ref, win_ref, poolw_ref,
                            pscale_ref, convw_ref, convb_ref, wbd_ref, ba_ref, bi_ref, lam_ref,
                            q_ref, k_ref, v_ref, a_ref, c_ref, ptail_ref, ctail_ref, hlast_ref,
                            pbuf, cbuf, *, n_batch, sq, start):
    rows = n_batch * sq
    u = _rms(x_ref[...], gpre_ref[...])
    z = jnp.dot(u.astype(BF16), win_ref[...], preferred_element_type=F32)
    bw = BRANCH_W
    xp = z[:, 0:bw]
    q_ref[...] = z[:, bw:2 * bw]
    k_ref[...] = z[:, 2 * bw:3 * bw]
    v_ref[...] = z[:, 3 * bw:4 * bw]
    xl = z[:, 4 * bw:5 * bw]
    gl = z[:, 5 * bw:6 * bw]

    pos = start + (lax.broadcasted_iota(jnp.int32, (rows, 1), 0) & (sq - 1))

    pbuf[:, 0:POOL_PAD, :] = phist_ref[...]
    pbuf[:, POOL_PAD:POOL_PAD + sq, :] = xp.reshape(n_batch, sq, bw)
    wins, cnts = [], []
    for g, w in enumerate(POOL_WINDOWS):
        c0 = g * POOL_GROUP_W
        win = pbuf[:, POOL_PAD:POOL_PAD + sq, c0:c0 + POOL_GROUP_W]
        for j in range(1, w):
            win = win + pbuf[:, POOL_PAD - j:POOL_PAD - j + sq, c0:c0 + POOL_GROUP_W]
        wins.append(win.reshape(rows, POOL_GROUP_W))
        cnts.append(jnp.minimum(pos + 1, w).astype(F32))
    a_ref[...] = _pool_mix(wins, xp, cnts, poolw_ref, pscale_ref[...]).astype(a_ref.dtype)
    ptail_ref[...] = pbuf[:, sq:sq + POOL_PAD, :]

    cbuf[:, 0:CONV_PAD, :] = chist_ref[...]
    cbuf[:, CONV_PAD:CONV_PAD + sq, :] = xl.reshape(n_batch, sq, bw)
    xc = xl.reshape(n_batch, sq, bw) * convw_ref[CONV_W - 1:CONV_W, :]
    for j in range(CONV_W - 1):
        off = CONV_PAD - (CONV_W - 1) + j
        xc = xc + cbuf[:, off:off + sq, :] * convw_ref[j:j + 1, :]
    xc = xc.reshape(rows, bw) + convb_ref[...]
    ctail_ref[...] = cbuf[:, sq:sq + CONV_PAD, :]

    a, uu = _lru_coeffs(xc, wbd_ref, ba_ref[...], bi_ref[...], lam_ref[...],
                        (pos == 0) if start == 0 else None)
    a, uu = _scan8(a, uu)
    h = a.reshape(n_batch, sq, bw) * h0_ref[...] + uu.reshape(n_batch, sq, bw)
    hlast_ref[...] = h[:, sq - 1:sq, :]
    c_ref[...] = (h.reshape(rows, bw) * jax.nn.gelu(gl)).astype(c_ref.dtype)


def _mixer_in_sample(x, n_batch, sq, start, phist, chist, h0, gpre, win, poolw, pscale, convw, convb,
                     wbd, ba, bi, lam):
    assert sq == SUBLANES, "sample sequences must fill exactly one sublane group"
    rows = n_batch * sq
    args = (x, phist, chist, h0, gpre, win, poolw, pscale, convw, convb, wbd, ba, bi, lam)
    return pl.pallas_call(
        functools.partial(_mixer_in_sample_kernel, n_batch=n_batch, sq=sq, start=start),
        grid=(1,),
        in_specs=[_const_spec(a.shape) for a in args],
        out_specs=[_const_spec((rows, BRANCH_W))] * 5
        + [_const_spec((n_batch, POOL_PAD, BRANCH_W)), _const_spec((n_batch, CONV_PAD, BRANCH_W)),
           _const_spec((n_batch, 1, BRANCH_W))],
        out_shape=[jax.ShapeDtypeStruct((rows, BRANCH_W), F32)] * 3
        + [jax.ShapeDtypeStruct((rows, BRANCH_W), BF16)] * 2
        + [jax.ShapeDtypeStruct((n_batch, POOL_PAD, BRANCH_W), F32),
           jax.ShapeDtypeStruct((n_batch, CONV_PAD, BRANCH_W), F32),
           jax.ShapeDtypeStruct((n_batch, 1, BRANCH_W), F32)],
        scratch_shapes=[pltpu.VMEM((n_batch, POOL_PAD + sq, BRANCH_W), F32),
                        pltpu.VMEM((n_batch, CONV_PAD + sq, BRANCH_W), F32)],
        compiler_params=_cparams(1),
        name="mixer_in_sample",
    )(*args)


def _topk_rows(g, n_valid, k):
    ridx = lax.broadcasted_iota(jnp.int32, g.shape, 0)
    valid = ridx < n_valid
    rows = []
    for j in range(n_valid):
        gj = g[j:j + 1, :]
        beats = jnp.where(ridx < j, jnp.where(g >= gj, 1.0, 0.0), jnp.where(g > gj, 1.0, 0.0))
        cnt = jnp.sum(jnp.where(valid, beats, 0.0), axis=0, keepdims=True)
        rows.append(cnt < float(k))
    return rows


def _moba_prompt_block(i, qt_ref, k_ref, vt_ref, means_ref, o_ref):
    blk = MOBA_BLOCK
    pair_w = 2 * HEAD_DIM
    nk = (i + 1) * blk
    qt = qt_ref[...]
    ch = lax.broadcasted_iota(jnp.int32, (pair_w, 1), 0)
    key_r = lax.broadcasted_iota(jnp.int32, (blk, blk), 0)
    qry_c = lax.broadcasted_iota(jnp.int32, (blk, blk), 1)
    causal = key_r <= qry_c
    scale = HEAD_DIM ** -0.5 * LOG2_E

    qhs = [jnp.where((ch >= h * HEAD_DIM) & (ch < (h + 1) * HEAD_DIM), qt, 0.0) for h in range(2)]
    qs = jnp.concatenate([(qh * scale).astype(BF16) for qh in qhs], axis=1)
    st = jnp.dot(k_ref[0:nk, :], qs, preferred_element_type=F32)
    causal2 = jnp.concatenate([causal, causal], axis=1)
    pieces = []
    if i > MOBA_TOPK:
        sels = [_topk_rows(_dot_split(means_ref[...], qh, _NN), i, MOBA_TOPK) for qh in qhs]
        for j in range(i):
            sel_j = jnp.concatenate([jnp.where(s[j], 1.0, 0.0) for s in sels], axis=1)
            pieces.append(jnp.where(sel_j > 0.5, st[j * blk:(j + 1) * blk], NEG_INF))
    else:
        pieces = [st[j * blk:(j + 1) * blk] for j in range(i)]
    pieces.append(jnp.where(causal2, st[i * blk:], NEG_INF))
    sm = jnp.concatenate(pieces, axis=0) if i else pieces[0]
    m = jnp.max(sm, axis=0, keepdims=True)
    p = jnp.exp2(sm - m).astype(BF16)
    vt1 = jnp.concatenate([vt_ref[:, 0:nk], jnp.ones((BF16_SUBLANES, nk), BF16)], axis=0)
    acc = jnp.dot(vt1, p, preferred_element_type=F32)
    out2 = acc[0:pair_w] / acc[pair_w:pair_w + 1]
    out_t = jnp.where(ch < HEAD_DIM, out2[:, 0:blk], out2[:, blk:])
    o_ref[...] = out_t.T.astype(o_ref.dtype)


def _moba_prompt_kernel(qt_ref, k_ref, vt_ref, means_ref, o_ref, *, nblk):
    i = pl.program_id(0)
    for c in range(nblk):
        pl.when(i == c)(functools.partial(_moba_prompt_block, c, qt_ref, k_ref, vt_ref, means_ref, o_ref))


def _moba_prompt(qt, kbf, vtb, means, n_batch, seq):
    assert seq % MOBA_BLOCK == 0
    nblk = seq // MOBA_BLOCK
    pair_w = 2 * HEAD_DIM
    n_pairs = BRANCH_W // pair_w
    out = pl.pallas_call(
        functools.partial(_moba_prompt_kernel, nblk=nblk),
        grid=(nblk, n_batch, n_pairs),
        in_specs=[pl.BlockSpec((None, pair_w, MOBA_BLOCK), lambda i, b, hp: (b, hp, i)),
                  pl.BlockSpec((None, seq, pair_w), lambda i, b, hp: (b, 0, hp)),
                  pl.BlockSpec((None, pair_w, seq), lambda i, b, hp: (b, hp, 0)),
                  pl.BlockSpec((None, nblk, pair_w), lambda i, b, hp: (b, 0, hp))],
        out_specs=pl.BlockSpec((None, MOBA_BLOCK, pair_w), lambda i, b, hp: (b, i, hp)),
        out_shape=jax.ShapeDtypeStruct((n_batch, seq, BRANCH_W), BF16),
        compiler_params=_cparams(3),
        name="moba_prompt",
    )(qt, kbf.reshape(n_batch, seq, BRANCH_W), vtb, means)
    return out.reshape(n_batch * seq, BRANCH_W)


CHUNK_PAGES = 8
DMA_SLOTS = 4


def _moba_sample_kernel(pt_ref, q_ref, kn_ref, vn_ref, ck_ref, cv_ref, o_ref,
                        buf, sem, s_sc, *, layer, n_batch, n_pages, sq):
    b = pl.program_id(0)
    ppb = MOBA_BLOCK // PAGE_SIZE
    nblk = n_pages // ppb
    n_chunks = n_pages // CHUNK_PAGES
    bpc = CHUNK_PAGES // ppb
    nrow = N_HEADS * sq

    def chunk_copies(bb, c, slot):
        src = ck_ref if c < n_chunks else cv_ref
        cc = c % n_chunks
        return [pltpu.make_async_copy(src.at[layer, pt_ref[bb, cc * CHUNK_PAGES + p]],
                                      buf.at[slot, p], sem.at[slot])
                for p in range(CHUNK_PAGES)]

    def start_chunk(bb, c):
        for cp in chunk_copies(bb, c, c % DMA_SLOTS):
            cp.start()

    def wait_chunk(bb, c):
        for cp in chunk_copies(bb, c, c % DMA_SLOTS):
            cp.wait()

    def prefetch(c):
        nxt = c + DMA_SLOTS - 1
        if nxt < 2 * n_chunks:
            start_chunk(b, nxt)
        else:
            @pl.when(b + 1 < n_batch)
            def _():
                start_chunk(b + 1, nxt - 2 * n_chunks)

    @pl.when(b == 0)
    def _():
        for c in range(DMA_SLOTS - 1):
            start_chunk(b, c)

    qb = q_ref[...]
    qrows = jnp.concatenate([qb] * N_HEADS, axis=0)
    rh = lax.broadcasted_iota(jnp.int32, (nrow, BRANCH_W), 0) >> (sq.bit_length() - 1)
    chh = lax.broadcasted_iota(jnp.int32, (nrow, BRANCH_W), 1) >> (HEAD_DIM.bit_length() - 1)
    head_mask = rh == chh
    qrows = jnp.where(head_mask, qrows, 0.0)
    qs = (qrows * (HEAD_DIM ** -0.5)).astype(BF16)

    def block_of(slot, n):
        return jnp.concatenate([buf[slot, pg] for pg in range(n * ppb, (n + 1) * ppb)], axis=1)

    lane_i = lax.broadcasted_iota(jnp.int32, (nrow, LANES), 1)
    gs = jnp.zeros((nrow, LANES), F32)
    for c in range(n_chunks):
        prefetch(c)
        wait_chunk(b, c)
        for n in range(bpc):
            blk_id = c * bpc + n
            s = jnp.dot(qs, block_of(c % DMA_SLOTS, n).astype(BF16), preferred_element_type=F32)
            s_sc[blk_id] = s
            gs = jnp.where(lane_i == blk_id, jnp.sum(s, axis=1, keepdims=True), gs)

    lane = lane_i.astype(F32)
    sel_f = jnp.zeros(gs.shape, F32)
    work = jnp.where(lane_i < nblk, gs, -jnp.inf)
    for _ in range(min(MOBA_TOPK, nblk)):
        mx = jnp.max(work, axis=1, keepdims=True)
        first = jnp.min(jnp.where(work == mx, lane, float(nblk)), axis=1, keepdims=True)
        pick = lane == first
        sel_f = jnp.where(pick, 1.0, sel_f)
        work = jnp.where(pick, -jnp.inf, work)

    s_own = lax.dot_general(qs, kn_ref[...].astype(BF16), _NT, preferred_element_type=F32)
    qi = lax.broadcasted_iota(jnp.int32, (nrow, sq), 0) & (sq - 1)
    kk = lax.broadcasted_iota(jnp.int32, (nrow, sq), 1)
    s_own = jnp.where(kk <= qi, s_own, NEG_INF)
    def masked(blk_id):
        return jnp.where(sel_f[:, blk_id:blk_id + 1] > 0.5, s_sc[blk_id], NEG_INF)

    m_run = masked(0)
    for n in range(1, nblk):
        m_run = jnp.maximum(m_run, masked(n))
    m = jnp.maximum(jnp.max(s_own, axis=1, keepdims=True), jnp.max(m_run, axis=1, keepdims=True))
    p_own = jnp.exp(s_own - m)
    acc = jnp.dot(p_own.astype(BF16), vn_ref[...].astype(BF16), preferred_element_type=F32)

    l_run = jnp.zeros((nrow, MOBA_BLOCK), F32)
    for c in range(n_chunks, 2 * n_chunks):
        prefetch(c)
        wait_chunk(b, c)
        for n in range(bpc):
            p = jnp.exp(masked((c - n_chunks) * bpc + n) - m)
            l_run = l_run + p
            acc = acc + lax.dot_general(p.astype(BF16), block_of(c % DMA_SLOTS, n).astype(BF16), _NT,
                                        preferred_element_type=F32)
    l = jnp.sum(p_own, axis=1, keepdims=True) + jnp.sum(l_run, axis=1, keepdims=True)

    full = jnp.where(head_mask, acc / l, 0.0)
    out = full[0:sq]
    for h in range(1, N_HEADS):
        out = out + full[h * sq:(h + 1) * sq]
    o_ref[...] = out.astype(o_ref.dtype)


def _moba_sample(q, k_new, v_new, cache_k, cache_v, page_table, layer, n_batch, sq):
    n_pages = page_table.shape[1]
    assert (n_pages * PAGE_SIZE) % MOBA_BLOCK == 0, "own block must hold only the new rows"
    assert n_pages % CHUNK_PAGES == 0 and (2 * n_pages // CHUNK_PAGES) % DMA_SLOTS == 0
    nblk = n_pages * PAGE_SIZE // MOBA_BLOCK
    assert nblk <= LANES, "block scores are kept one lane per block"
    depth, n_pool = cache_k.shape[0], cache_k.shape[1]
    ck = cache_k.transpose(0, 1, 3, 4, 2).reshape(depth, n_pool, BRANCH_W, PAGE_SIZE)
    cv = cache_v.transpose(0, 1, 3, 4, 2).reshape(depth, n_pool, BRANCH_W, PAGE_SIZE)
    q3, k3, v3 = (t.reshape(n_batch, sq, BRANCH_W) for t in (q, k_new, v_new))
    row = pl.BlockSpec((None, sq, BRANCH_W), lambda b, pt: (b, 0, 0))
    hbm = pl.BlockSpec(memory_space=pl.ANY)
    out = pl.pallas_call(
        functools.partial(_moba_sample_kernel, layer=layer, n_batch=n_batch, n_pages=n_pages, sq=sq),
        grid_spec=pltpu.PrefetchScalarGridSpec(
            num_scalar_prefetch=1,
            grid=(n_batch,),
            in_specs=[row, row, row, hbm, hbm],
            out_specs=row,
            scratch_shapes=[pltpu.VMEM((DMA_SLOTS, CHUNK_PAGES, BRANCH_W, PAGE_SIZE), F32),
                            pltpu.SemaphoreType.DMA((DMA_SLOTS,)),
                            pltpu.VMEM((nblk, N_HEADS * sq, MOBA_BLOCK), F32)]),
        out_shape=jax.ShapeDtypeStruct((n_batch, sq, BRANCH_W), BF16),
        compiler_params=_cparams(1),
        name="moba_sample",
    )(page_table, q3, k3, v3, ck, cv)
    return out.reshape(n_batch * sq, BRANCH_W)


def _merge_kernel(x_ref, a_ref, b_ref, c_ref, gpre_ref, gpost_ref, wg_ref, proj_ref, wo_ref, o_ref):
    x = x_ref[...]
    u = _rms(x, gpre_ref[...]).astype(BF16)
    gates = jax.nn.sigmoid(jnp.dot(u, wg_ref[...], preferred_element_type=F32))
    merged = None
    for n, br in enumerate((a_ref, b_ref, c_ref)):
        t = gates[:, n * D_MODEL:(n + 1) * D_MODEL] * jnp.dot(br[...], proj_ref[n], preferred_element_type=F32)
        merged = t if merged is None else merged + t
    m = jnp.dot(merged.astype(BF16), wo_ref[...], preferred_element_type=F32)
    o_ref[...] = x + _rms(m, gpost_ref[...])


def _merge(x, a, b, c, gpre, gpost, wg, proj, wo, tm):
    n = x.shape[0]
    row = lambda w: pl.BlockSpec((tm, w), lambda i: (i, 0))
    consts = (gpre, gpost, wg, proj, wo)
    return pl.pallas_call(
        _merge_kernel,
        grid=(n // tm,),
        in_specs=[row(D_MODEL)] + [row(BRANCH_W)] * 3 + [_const_spec(t.shape) for t in consts],
        out_specs=row(D_MODEL),
        out_shape=jax.ShapeDtypeStruct((n, D_MODEL), F32),
        compiler_params=_cparams(1),
        name="merge",
    )(x, a, b, c, *consts)


def _ffn_dense_kernel(x_ref, gpre_ref, gpost_ref, wg_ref, wu_ref, wd_ref, o_ref, u_sc, acc_sc):
    f = pl.program_id(1)

    @pl.when(f == 0)
    def _():
        u_sc[...] = _rms(x_ref[...], gpre_ref[...]).astype(BF16)
        acc_sc[...] = jnp.zeros(acc_sc.shape, F32)

    u = u_sc[...]
    hg = jnp.dot(u, wg_ref[...], preferred_element_type=F32)
    hu = jnp.dot(u, wu_ref[...], preferred_element_type=F32)
    acc_sc[...] += jnp.dot((jax.nn.silu(hg) * hu).astype(BF16), wd_ref[...], preferred_element_type=F32)

    @pl.when(f == pl.num_programs(1) - 1)
    def _():
        o_ref[...] = x_ref[...] + _rms(acc_sc[...], gpost_ref[...])


def _ffn_dense(x, gpre, gpost, wg, wu, wd, tm, fc):
    n = x.shape[0]
    dff = wg.shape[1]
    row = pl.BlockSpec((tm, D_MODEL), lambda i, f: (i, 0))
    mode = dict(pipeline_mode=pl.Buffered(1)) if fc == dff else {}
    return pl.pallas_call(
        _ffn_dense_kernel,
        grid=(n // tm, dff // fc),
        in_specs=[row, _const_spec(gpre.shape), _const_spec(gpost.shape),
                  pl.BlockSpec((D_MODEL, fc), lambda i, f: (0, f), **mode),
                  pl.BlockSpec((D_MODEL, fc), lambda i, f: (0, f), **mode),
                  pl.BlockSpec((fc, D_MODEL), lambda i, f: (f, 0), **mode)],
        out_specs=row,
        out_shape=jax.ShapeDtypeStruct((n, D_MODEL), F32),
        scratch_shapes=[pltpu.VMEM((tm, D_MODEL), BF16), pltpu.VMEM((tm, D_MODEL), F32)],
        compiler_params=_cparams(2),
        name="ffn_dense",
    )(x, gpre, gpost, wg, wu, wd)


def _router_gate(u, router_ref):
    logits = _dot_split(u, router_ref[...], _NN)
    lane = lax.broadcasted_iota(jnp.int32, logits.shape, 1)
    logits = jnp.where(lane < N_EXPERTS, logits, -jnp.inf)
    m1 = jnp.max(logits, axis=1, keepdims=True)
    i1 = jnp.min(jnp.where(logits == m1, lane, LANES), axis=1, keepdims=True)
    rest = jnp.where(lane == i1, -jnp.inf, logits)
    m2 = jnp.max(rest, axis=1, keepdims=True)
    i2 = jnp.min(jnp.where(rest == m2, lane, LANES), axis=1, keepdims=True)
    e2 = jnp.exp(m2 - m1)
    w1 = 1.0 / (1.0 + e2)
    w2 = e2 / (1.0 + e2)
    gate = jnp.where(lane == i1, w1, 0.0) + jnp.where(lane == i2, w2, 0.0)
    chosen = jnp.where((lane == i1) | (lane == i2), 1.0, 0.0)
    return gate, chosen


MOE_CHUNK = 128
MOE_SLAB = 256


def _moe_kernel(x_ref, gpre_ref, gpost_ref, router_ref, wg_ref, wu_ref, wd_ref, o_ref,
                u_sc, gate_sc, rank_sc, rankt_sc, cnt_sc, xs_sc, y_sc, *, tm):
    e = pl.program_id(1)
    f = pl.program_id(2)
    nf = pl.num_programs(2)
    ck = MOE_CHUNK

    @pl.when((e == 0) & (f == 0))
    def _():
        u = _rms(x_ref[...], gpre_ref[...])
        u_sc[...] = u.astype(BF16)
        gate, chosen = _router_gate(u, router_ref)
        gate_sc[...] = gate
        earlier = (lax.broadcasted_iota(jnp.int32, (tm, tm), 1) < lax.broadcasted_iota(jnp.int32, (tm, tm), 0))
        rank = jnp.dot(jnp.where(earlier, 1.0, 0.0).astype(BF16), chosen.astype(BF16), preferred_element_type=F32)
        rank = jnp.where(chosen > 0.5, rank, -1.0)
        rank_sc[...] = rank
        rankt_sc[...] = rank.T[0:SUBLANES, :]
        cnt_sc[...] = jnp.sum(chosen, axis=0, keepdims=True)
        o_ref[...] = jnp.zeros(o_ref.shape, F32)
        y_sc[...] = jnp.zeros(y_sc.shape, F32)

    lane1 = lax.broadcasted_iota(jnp.int32, (1, LANES), 1)
    cnt = jnp.sum(jnp.where(lane1 == e, cnt_sc[...], 0.0))
    n_chunks = jnp.ceil(cnt * (1.0 / ck)).astype(jnp.int32)

    @pl.when(f == 0)
    def _():
        rank_row = rankt_sc[pl.ds(e, 1), :]
        slot = lax.broadcasted_iota(jnp.int32, (ck, 1), 0).astype(F32)

        def compact(c, carry):
            base = pl.multiple_of(c * ck, ck)
            onehot = jnp.where(rank_row - (c * ck).astype(F32) == slot, 1.0, 0.0).astype(BF16)
            xs_sc[pl.ds(base, ck), :] = jnp.dot(onehot, u_sc[...], preferred_element_type=F32).astype(BF16)
            return carry

        lax.fori_loop(0, n_chunks, compact, 0)

    def expert(c, carry):
        base = pl.multiple_of(c * ck, ck)
        xs = xs_sc[pl.ds(base, ck), :]
        hg = jnp.dot(xs, wg_ref[...], preferred_element_type=F32)
        hu = jnp.dot(xs, wu_ref[...], preferred_element_type=F32)
        y = jnp.dot((jax.nn.silu(hg) * hu).astype(BF16), wd_ref[...], preferred_element_type=F32)

        @pl.when(f == 0)
        def _():
            y_sc[pl.ds(base, ck), :] = y

        @pl.when(f > 0)
        def _():
            y_sc[pl.ds(base, ck), :] += y

        return carry

    lax.fori_loop(0, n_chunks, expert, 0)

    @pl.when(f == nf - 1)
    def _():
        ck2 = 2 * ck
        lane = lax.broadcasted_iota(jnp.int32, (MOE_SLAB, LANES), 1)
        slot = lax.broadcasted_iota(jnp.int32, (1, ck2), 1).astype(F32)

        def scatter(c, carry):
            base = pl.multiple_of(c * ck2, ck2)
            yb = y_sc[pl.ds(base, ck2), :].astype(BF16)
            for s0 in range(0, tm, MOE_SLAB):
                rows = slice(s0, s0 + MOE_SLAB)
                rank_col = jnp.sum(jnp.where(lane == e, rank_sc[rows, :], 0.0), axis=1, keepdims=True)
                gate_col = jnp.sum(jnp.where(lane == e, gate_sc[rows, :], 0.0), axis=1, keepdims=True)
                onehot_t = jnp.where(rank_col - (c * ck2).astype(F32) == slot, 1.0, 0.0).astype(BF16)
                o_ref[rows, :] += gate_col * jnp.dot(onehot_t, yb, preferred_element_type=F32)
            return carry

        lax.fori_loop(0, (n_chunks + 1) // 2, scatter, 0)

    @pl.when((e == pl.num_programs(1) - 1) & (f == nf - 1))
    def _():
        o_ref[...] = x_ref[...] + _rms(o_ref[...], gpost_ref[...])


def _moe(x, gpre, gpost, router_pad, wg, wu, wd, tm, fc):
    n = x.shape[0]
    n_exp, _, dff = wg.shape
    assert n_exp <= SUBLANES and tm % MOE_SLAB == 0 and MOE_SLAB % MOE_CHUNK == 0
    cap = pl.cdiv(tm, 2 * MOE_CHUNK) * 2 * MOE_CHUNK
    row = pl.BlockSpec((tm, D_MODEL), lambda i, e, f: (i, 0))
    return pl.pallas_call(
        functools.partial(_moe_kernel, tm=tm),
        grid=(n // tm, n_exp, dff // fc),
        in_specs=[row, _const_spec(gpre.shape), _const_spec(gpost.shape), _const_spec(router_pad.shape),
                  pl.BlockSpec((None, D_MODEL, fc), lambda i, e, f: (e, 0, f)),
                  pl.BlockSpec((None, D_MODEL, fc), lambda i, e, f: (e, 0, f)),
                  pl.BlockSpec((None, fc, D_MODEL), lambda i, e, f: (e, f, 0))],
        out_specs=row,
        out_shape=jax.ShapeDtypeStruct((n, D_MODEL), F32),
        scratch_shapes=[pltpu.VMEM((tm, D_MODEL), BF16), pltpu.VMEM((tm, LANES), F32),
                        pltpu.VMEM((tm, LANES), F32), pltpu.VMEM((SUBLANES, tm), F32),
                        pltpu.VMEM((1, LANES), F32),
                        pltpu.VMEM((cap, D_MODEL), BF16), pltpu.VMEM((cap, D_MODEL), F32)],
        compiler_params=_cparams(3),
        name="moe",
    )(x, gpre, gpost, router_pad, wg, wu, wd)


def _block_diag_halves(w_a, w_i):
    half_blocks = LRU_BLOCKS // 2

    def bd(w4):
        eye = jnp.eye(half_blocks, dtype=w4.dtype)
        return jnp.einsum('hcd,hg->hcgd', w4, eye).reshape(half_blocks * LRU_BLOCK_W, half_blocks * LRU_BLOCK_W)

    halves = [jnp.concatenate([bd(w_a[i * half_blocks:(i + 1) * half_blocks]),
                               bd(w_i[i * half_blocks:(i + 1) * half_blocks])], axis=1) for i in range(2)]
    return jnp.stack(halves).astype(BF16)


def _row(v):
    return v.reshape(1, -1)


def _pad_hist(h, pad_rows):
    return jnp.pad(h, ((0, 0), (pad_rows - h.shape[1], 0), (0, 0)))


PROMPT_TS = 512
ROW_TM = 512
FFN_FC = 2816
MOE_FC = 1792
MOE_TM = 1024


def kernel(x_prompt, x_sample, cache_k, cache_v, page_table, state_pool, state_conv, state_h, norm_mix_pre, norm_mix_post, norm_ffn_pre, norm_ffn_post, w_in, pool_w, pool_scale, conv_w, conv_b, lru_wa, lru_ba, lru_wi, lru_bi, lru_lambda, proj_pool, proj_attn, proj_lru, w_o, ffn_w_gate, ffn_w_up, ffn_w_down, router_w, moe_w_gate, moe_w_up, moe_w_down):
    depth = w_in.shape[0]
    bp, sp, _ = x_prompt.shape
    bs, ss, _ = x_sample.shape
    past_len = page_table.shape[1] * PAGE_SIZE
    xp = x_prompt.reshape(bp * sp, D_MODEL)
    xs = x_sample.reshape(bs * ss, D_MODEL)
    tm_s = bs * ss

    outs = {name: [] for name in ("ks", "vs", "pp", "ps", "cp", "cs", "hp", "hs")}
    kv_pages = tuple(jnp.zeros((depth, bp, sp // PAGE_SIZE, BRANCH_W, PAGE_SIZE), F32) for _ in range(2))
    for l in range(depth):
        w_qkv = w_in[l, :, :D_QKV].astype(BF16)
        w_gates = w_in[l, :, D_QKV:].astype(BF16)
        mix_consts = (_row(norm_mix_pre[l]), w_qkv, pool_w[l].astype(BF16), _row(pool_scale[l]),
                      conv_w[l], _row(conv_b[l]), _block_diag_halves(lru_wa[l], lru_wi[l]),
                      _row(lru_ba[l]), _row(lru_bi[l]), _row(lru_lambda[l]))
        proj = jnp.stack([proj_pool[l], proj_attn[l], proj_lru[l]]).astype(BF16)
        merge_consts = (_row(norm_mix_pre[l]), _row(norm_mix_post[l]), w_gates, proj, w_o[l].astype(BF16))

        qt_p, kbf_p, vtb_p, kmean_p, kpages, vpages, a_p, c_p, pt_p, ct_p, h_p = _mixer_in_prompt(
            xp, bp, sp, PROMPT_TS, l, depth, kv_pages, *mix_consts)
        kv_pages = (kpages, vpages)
        b_p = _moba_prompt(qt_p, kbf_p, vtb_p, kmean_p.reshape(bp, sp // MOBA_BLOCK, BRANCH_W), bp, sp)
        xp = _merge(xp, a_p, b_p, c_p, *merge_consts, ROW_TM)

        q_s, k_s, v_s, a_s, c_s, pt_s, ct_s, h_s = _mixer_in_sample(
            xs, bs, ss, past_len, _pad_hist(state_pool[l], POOL_PAD), _pad_hist(state_conv[l], CONV_PAD),
            state_h[l].reshape(bs, 1, BRANCH_W), *mix_consts)
        b_s = _moba_sample(q_s, k_s, v_s, cache_k, cache_v, page_table, l, bs, ss)
        xs = _merge(xs, a_s, b_s, c_s, *merge_consts, tm_s)

        gpre, gpost = _row(norm_ffn_pre[l]), _row(norm_ffn_post[l])
        j = l // 2
        if l % 2 == 0:
            wg, wu, wd = (w[j].astype(BF16) for w in (ffn_w_gate, ffn_w_up, ffn_w_down))
            xp = _ffn_dense(xp, gpre, gpost, wg, wu, wd, ROW_TM, FFN_FC)
            xs = _ffn_dense(xs, gpre, gpost, wg, wu, wd, tm_s, FFN_FC)
        else:
            wg, wu, wd = (w[j].astype(BF16) for w in (moe_w_gate, moe_w_up, moe_w_down))
            router_pad = jnp.pad(router_w[j], ((0, 0), (0, LANES - N_EXPERTS)))
            xp = _moe(xp, gpre, gpost, router_pad, wg, wu, wd, MOE_TM, MOE_FC)
            xs = _moe(xs, gpre, gpost, router_pad, wg, wu, wd, tm_s, MOE_FC)

        outs["ks"].append(k_s.reshape(bs, ss, N_HEADS, HEAD_DIM))
        outs["vs"].append(v_s.reshape(bs, ss, N_HEADS, HEAD_DIM))
        outs["pp"].append(pt_p[:, POOL_PAD - POOL_HIST:])
        outs["ps"].append(pt_s[:, POOL_PAD - POOL_HIST:])
        outs["cp"].append(ct_p[:, CONV_PAD - (CONV_W - 1):])
        outs["cs"].append(ct_s[:, CONV_PAD - (CONV_W - 1):])
        outs["hp"].append(h_p.reshape(bp, BRANCH_W))
        outs["hs"].append(h_s.reshape(bs, BRANCH_W))

    st = {k: jnp.stack(v) for k, v in outs.items()}
    kp, vp = (t.reshape(depth, bp, sp // PAGE_SIZE, N_HEADS, HEAD_DIM, PAGE_SIZE).transpose(0, 1, 2, 5, 3, 4)
              for t in kv_pages)
    return (xp.reshape(bp, sp, D_MODEL), xs.reshape(bs, ss, D_MODEL),
            kp, vp, st["ks"], st["vs"], st["pp"], st["ps"], st["cp"], st["cs"], st["hp"], st["hs"])
```

```python
import functools

import jax
import jax.numpy as jnp
from jax import lax
from jax.experimental import pallas as pl
from jax.experimental.pallas import tpu as pltpu

F32 = jnp.float32
BF16 = jnp.bfloat16

D_MODEL = 1024
BRANCH_W = 512
POOL_WINDOWS = (2, 4, 8, 16)
POOL_GROUP_W = BRANCH_W // len(POOL_WINDOWS)
POOL_HIST = max(POOL_WINDOWS) - 1
POOL_PAD = 16
N_HEADS = 8
HEAD_DIM = BRANCH_W // N_HEADS
MOBA_BLOCK = 256
MOBA_TOPK = 3
PAGE_SIZE = 128
LRU_BLOCKS = 8
LRU_BLOCK_W = BRANCH_W // LRU_BLOCKS
CONV_W = 4
CONV_PAD = 8
LRU_C = 8.0
N_EXPERTS = 8
TOP_K = 2
EPS = 1e-6
NEG_INF = -1e30
LOG2_E = 1.4426950408889634
D_QKV = 6 * BRANCH_W
SUBLANES = 8
BF16_SUBLANES = 16
MXU_DEPTH = 256
LANES = 128
VMEM_LIMIT = 56 * 1024 * 1024


def _cparams(n_axes):
    return pltpu.CompilerParams(
        dimension_semantics=("arbitrary",) * n_axes, vmem_limit_bytes=VMEM_LIMIT)


def _rms(x, g):
    return x * lax.rsqrt(jnp.mean(x * x, axis=-1, keepdims=True) + EPS) * g


def _split_bf16(x):
    hi = x.astype(BF16)
    lo = (x - hi.astype(F32)).astype(BF16)
    return hi, lo


def _dot_split(a, b, dims):
    a_hi, a_lo = _split_bf16(a)
    b_hi, b_lo = _split_bf16(b)
    d = functools.partial(lax.dot_general, dimension_numbers=dims, preferred_element_type=F32)
    return d(a_hi, b_hi) + (d(a_hi, b_lo) + d(a_lo, b_hi))


_NN = (((1,), (0,)), ((), ()))
_NT = (((1,), (1,)), ((), ()))


def _sigmoid(x):
    return 0.5 * jnp.tanh(0.5 * x) + 0.5


def _softplus(x):
    return jnp.maximum(x, 0.0) + jnp.log1p(jnp.exp(-jnp.abs(x)))


def _lru_coeffs(xc, wbd_ref, ba, bi, lam, first_row_is_start):
    xcb = xc.astype(BF16)
    half = BRANCH_W // 2
    g0 = jnp.dot(xcb[:, :half], wbd_ref[0], preferred_element_type=F32)
    g1 = jnp.dot(xcb[:, half:], wbd_ref[1], preferred_element_type=F32)
    pre_a = jnp.concatenate([g0[:, :half], g1[:, :half]], axis=1)
    pre_i = jnp.concatenate([g0[:, half:], g1[:, half:]], axis=1)
    r = _sigmoid(pre_a + ba)
    i = _sigmoid(pre_i + bi)
    log_a = (-LRU_C) * r * _softplus(-lam)
    a = jnp.exp(log_a)
    t = jnp.tanh(log_a)
    mult = jnp.sqrt(-2.0 * t) * lax.rsqrt(1.0 - t)
    if first_row_is_start is not None:
        mult = jnp.where(first_row_is_start, 1.0, mult)
    return a, mult * i * xc


def _scan8(a, u):
    rows, width = a.shape
    a = a.reshape(rows // SUBLANES, SUBLANES, width)
    u = u.reshape(rows // SUBLANES, SUBLANES, width)
    r8 = lax.broadcasted_iota(jnp.int32, a.shape, 1)
    for d in (1, 2, 4):
        a_s = pltpu.roll(a, d, axis=1)
        u_s = pltpu.roll(u, d, axis=1)
        m = r8 >= d
        u = jnp.where(m, u + a * u_s, u)
        a = jnp.where(m, a * a_s, a)
    return a.reshape(rows, width), u.reshape(rows, width)


def _pool_mix(win_groups, xp, cnt_groups, poolw_ref, pscale):
    outs = []
    for g in range(len(POOL_WINDOWS)):
        c0 = g * POOL_GROUP_W
        pooled = win_groups[g] / cnt_groups[g] - xp[:, c0:c0 + POOL_GROUP_W]
        outs.append(jnp.dot(pooled.astype(BF16), poolw_ref[g], preferred_element_type=F32))
    return jnp.concatenate(outs, axis=1) * pscale


def _mixer_in_prompt_kernel(x_ref, gpre_ref, win_ref, poolw_ref, pscale_ref, convw_ref, convb_ref,
                            wbd_ref, ba_ref, bi_ref, lam_ref, *rest, ts):
    (qt_ref, kbf_ref, vtb_ref, kmean_ref, kpage_ref, vpage_ref, a_ref, c_ref,
     ptail_ref, ctail_ref, hlast_ref, pbuf, cbuf, hcar) = rest[-14:]
    s = pl.program_id(1)

    @pl.when(s == 0)
    def _():
        pbuf[0:POOL_PAD, :] = jnp.zeros((POOL_PAD, BRANCH_W), F32)
        cbuf[0:CONV_PAD, :] = jnp.zeros((CONV_PAD, BRANCH_W), F32)
        hcar[...] = jnp.zeros((1, BRANCH_W), F32)

    u = _rms(x_ref[...], gpre_ref[...])
    z = jnp.dot(u.astype(BF16), win_ref[...], preferred_element_type=F32)
    bw = BRANCH_W
    xp = z[:, 0:bw]
    k = z[:, 2 * bw:3 * bw]
    xl = z[:, 4 * bw:5 * bw]
    gl = z[:, 5 * bw:6 * bw]

    qt_ref[...] = z[:, bw:2 * bw].T
    kbf_ref[...] = k.astype(BF16)
    kt = k.T
    vt = z[:, 3 * bw:4 * bw].T
    vtb_ref[...] = vt.astype(BF16)
    for n in range(ts // MOBA_BLOCK):
        kmean_ref[n:n + 1, :] = jnp.sum(k[n * MOBA_BLOCK:(n + 1) * MOBA_BLOCK], axis=0, keepdims=True) * (1.0 / MOBA_BLOCK)
    for p in range(ts // PAGE_SIZE):
        kpage_ref[p] = kt[:, p * PAGE_SIZE:(p + 1) * PAGE_SIZE]
        vpage_ref[p] = vt[:, p * PAGE_SIZE:(p + 1) * PAGE_SIZE]

    pos = s * ts + lax.broadcasted_iota(jnp.int32, (ts, 1), 0)

    pbuf[POOL_PAD:POOL_PAD + ts, :] = xp
    wins, cnts = [], []
    for g, w in enumerate(POOL_WINDOWS):
        c0 = g * POOL_GROUP_W
        win = pbuf[:, c0:c0 + POOL_GROUP_W]
        d = 1
        while d < w:
            win = win + pltpu.roll(win, d, axis=0)
            d *= 2
        wins.append(win[POOL_PAD:])
        cnts.append(jnp.minimum(pos + 1, w).astype(F32))
    a_ref[...] = _pool_mix(wins, xp, cnts, poolw_ref, pscale_ref[...]).astype(a_ref.dtype)
    tail = pbuf[ts:ts + POOL_PAD, :]
    pbuf[0:POOL_PAD, :] = tail
    ptail_ref[...] = tail

    cbuf[CONV_PAD:CONV_PAD + ts, :] = xl
    xc = convb_ref[...] + xl * convw_ref[CONV_W - 1:CONV_W, :]
    for j in range(CONV_W - 1):
        off = CONV_PAD - (CONV_W - 1) + j
        xc = xc + cbuf[off:off + ts, :] * convw_ref[j:j + 1, :]
    ctail = cbuf[ts:ts + CONV_PAD, :]
    cbuf[0:CONV_PAD, :] = ctail
    ctail_ref[...] = ctail

    a, uu = _lru_coeffs(xc, wbd_ref, ba_ref[...], bi_ref[...], lam_ref[...], pos == 0)
    a, uu = _scan8(a, uu)
    carry = jnp.broadcast_to(hcar[...], (SUBLANES, BRANCH_W))
    hs = []
    for g in range(ts // SUBLANES):
        hg = a[g * SUBLANES:(g + 1) * SUBLANES] * carry + uu[g * SUBLANES:(g + 1) * SUBLANES]
        hs.append(hg)
        carry = jnp.broadcast_to(hg[SUBLANES - 1:SUBLANES, :], (SUBLANES, BRANCH_W))
    h = jnp.concatenate(hs, axis=0)
    hcar[...] = carry[0:1, :]
    hlast_ref[...] = carry[0:1, :]
    c_ref[...] = (h * jax.nn.gelu(gl)).astype(c_ref.dtype)


def _const_spec(shape):
    nd = len(shape)
    return pl.BlockSpec(shape, lambda *_: (0,) * nd)


def _mixer_in_prompt(x, n_batch, seq, ts, layer, depth, kv_pages, gpre, win, poolw, pscale, convw, convb,
                     wbd, ba, bi, lam):
    assert ts % MOBA_BLOCK == 0 and seq % ts == 0
    n = n_batch * seq
    nt = seq // ts
    ppt = ts // PAGE_SIZE
    bpt = ts // MOBA_BLOCK
    row = lambda w: pl.BlockSpec((ts, w), lambda b, s: (b * nt + s, 0))
    chan = pl.BlockSpec((None, BRANCH_W, ts), lambda b, s: (b, 0, s))
    per_b = lambda r: pl.BlockSpec((None, r, BRANCH_W), lambda b, s: (b, 0, 0))
    pages = pl.BlockSpec((None, None, ppt, BRANCH_W, PAGE_SIZE), lambda b, s: (layer, b, s, 0, 0))
    page_shape = jax.ShapeDtypeStruct((depth, n_batch, seq // PAGE_SIZE, BRANCH_W, PAGE_SIZE), F32)
    consts = (gpre, win, poolw, pscale, convw, convb, wbd, ba, bi, lam)
    extra = tuple(kv_pages)
    assert all(t.shape == page_shape.shape and t.dtype == page_shape.dtype for t in extra)
    n_in = 1 + len(consts)
    return pl.pallas_call(
        functools.partial(_mixer_in_prompt_kernel, ts=ts),
        grid=(n_batch, nt),
        in_specs=[row(D_MODEL)] + [_const_spec(c.shape) for c in consts]
        + [pl.BlockSpec(memory_space=pl.ANY)] * len(extra),
        out_specs=[chan, row(BRANCH_W), chan,
                   pl.BlockSpec((None, None, bpt, BRANCH_W), lambda b, s: (b, s, 0, 0)),
                   pages, pages, row(BRANCH_W), row(BRANCH_W),
                   per_b(POOL_PAD), per_b(CONV_PAD), per_b(1)],
        out_shape=[jax.ShapeDtypeStruct((n_batch, BRANCH_W, seq), F32),
                   jax.ShapeDtypeStruct((n, BRANCH_W), BF16),
                   jax.ShapeDtypeStruct((n_batch, BRANCH_W, seq), BF16),
                   jax.ShapeDtypeStruct((n_batch, nt, bpt, BRANCH_W), F32),
                   page_shape, page_shape,
                   jax.ShapeDtypeStruct((n, BRANCH_W), BF16),
                   jax.ShapeDtypeStruct((n, BRANCH_W), BF16),
                   jax.ShapeDtypeStruct((n_batch, POOL_PAD, BRANCH_W), F32),
                   jax.ShapeDtypeStruct((n_batch, CONV_PAD, BRANCH_W), F32),
                   jax.ShapeDtypeStruct((n_batch, 1, BRANCH_W), F32)],
        scratch_shapes=[pltpu.VMEM((POOL_PAD + ts, BRANCH_W), F32),
                        pltpu.VMEM((CONV_PAD + ts, BRANCH_W), F32),
                        pltpu.VMEM((1, BRANCH_W), F32)],
        input_output_aliases={n_in + i: 4 + i for i in range(len(extra))},
        compiler_params=_cparams(2),
        name="mixer_in_prompt",
    )(x, *consts, *extra)


def _mixer_in_sample_kernel(x_ref, phist_ref, chist_ref, h0_ref, gpre_ref, win_ref, poolw_ref,
                            pscale_ref, convw_ref, convb_ref, wbd_ref, ba_ref, bi_ref, lam_ref,
                            q_ref, k_ref, v_ref, a_ref, c_ref, ptail_ref, ctail_ref, hlast_ref,
                            pbuf, cbuf, *, n_batch, sq, start):
    rows = n_batch * sq
    u = _rms(x_ref[...], gpre_ref[...])
    z = jnp.dot(u.astype(BF16), win_ref[...], preferred_element_type=F32)
    bw = BRANCH_W
    xp = z[:, 0:bw]
    q_ref[...] = z[:, bw:2 * bw]
    k_ref[...] = z[:, 2 * bw:3 * bw]
    v_ref[...] = z[:, 3 * bw:4 * bw]
    xl = z[:, 4 * bw:5 * bw]
    gl = z[:, 5 * bw:6 * bw]

    pos = start + (lax.broadcasted_iota(jnp.int32, (rows, 1), 0) & (sq - 1))

    pbuf[:, 0:POOL_PAD, :] = phist_ref[...]
    pbuf[:, POOL_PAD:POOL_PAD + sq, :] = xp.reshape(n_batch, sq, bw)
    wins, cnts = [], []
    for g, w in enumerate(POOL_WINDOWS):
        c0 = g * POOL_GROUP_W
        win = pbuf[:, POOL_PAD:POOL_PAD + sq, c0:c0 + POOL_GROUP_W]
        for j in range(1, w):
            win = win + pbuf[:, POOL_PAD - j:POOL_PAD - j + sq, c0:c0 + POOL_GROUP_W]
        wins.append(win.reshape(rows, POOL_GROUP_W))
        cnts.append(jnp.minimum(pos + 1, w).astype(F32))
    a_ref[...] = _pool_mix(wins, xp, cnts, poolw_ref, pscale_ref[...]).astype(a_ref.dtype)
    ptail_ref[...] = pbuf[:, sq:sq + POOL_PAD, :]

    cbuf[:, 0:CONV_PAD, :] = chist_ref[...]
    cbuf[:, CONV_PAD:CONV_PAD + sq, :] = xl.reshape(n_batch, sq, bw)
    xc = xl.reshape(n_batch, sq, bw) * convw_ref[CONV_W - 1:CONV_W, :]
    for j in range(CONV_W - 1):
        off = CONV_PAD - (CONV_W - 1) + j
        xc = xc + cbuf[:, off:off + sq, :] * convw_ref[j:j + 1, :]
    xc = xc.reshape(rows, bw) + convb_ref[...]
    ctail_ref[...] = cbuf[:, sq:sq + CONV_PAD, :]

    a, uu = _lru_coeffs(xc, wbd_ref, ba_ref[...], bi_ref[...], lam_ref[...],
                        (pos == 0) if start == 0 else None)
    a, uu = _scan8(a, uu)
    h = a.reshape(n_batch, sq, bw) * h0_ref[...] + uu.reshape(n_batch, sq, bw)
    hlast_ref[...] = h[:, sq - 1:sq, :]
    c_ref[...] = (h.reshape(rows, bw) * jax.nn.gelu(gl)).astype(c_ref.dtype)


def _mixer_in_sample(x, n_batch, sq, start, phist, chist, h0, gpre, win, poolw, pscale, convw, convb,
                     wbd, ba, bi, lam):
    assert sq == SUBLANES, "sample sequences must fill exactly one sublane group"
    rows = n_batch * sq
    args = (x, phist, chist, h0, gpre, win, poolw, pscale, convw, convb, wbd, ba, bi, lam)
    return pl.pallas_call(
        functools.partial(_mixer_in_sample_kernel, n_batch=n_batch, sq=sq, start=start),
        grid=(1,),
        in_specs=[_const_spec(a.shape) for a in args],
        out_specs=[_const_spec((rows, BRANCH_W))] * 5
        + [_const_spec((n_batch, POOL_PAD, BRANCH_W)), _const_spec((n_batch, CONV_PAD, BRANCH_W)),
           _const_spec((n_batch, 1, BRANCH_W))],
        out_shape=[jax.ShapeDtypeStruct((rows, BRANCH_W), F32)] * 3
        + [jax.ShapeDtypeStruct((rows, BRANCH_W), BF16)] * 2
        + [jax.ShapeDtypeStruct((n_batch, POOL_PAD, BRANCH_W), F32),
           jax.ShapeDtypeStruct((n_batch, CONV_PAD, BRANCH_W), F32),
           jax.ShapeDtypeStruct((n_batch, 1, BRANCH_W), F32)],
        scratch_shapes=[pltpu.VMEM((n_batch, POOL_PAD + sq, BRANCH_W), F32),
                        pltpu.VMEM((n_batch, CONV_PAD + sq, BRANCH_W), F32)],
        compiler_params=_cparams(1),
        name="mixer_in_sample",
    )(*args)


def _topk_rows(g, n_valid, k):
    ridx = lax.broadcasted_iota(jnp.int32, g.shape, 0)
    valid = ridx < n_valid
    rows = []
    for j in range(n_valid):
        gj = g[j:j + 1, :]
        beats = jnp.where(ridx < j, jnp.where(g >= gj, 1.0, 0.0), jnp.where(g > gj, 1.0, 0.0))
        cnt = jnp.sum(jnp.where(valid, beats, 0.0), axis=0, keepdims=True)
        rows.append(cnt < float(k))
    return rows


def _moba_prompt_block(i, qt_ref, k_ref, vt_ref, means_ref, o_ref):
    blk = MOBA_BLOCK
    pair_w = 2 * HEAD_DIM
    nk = (i + 1) * blk
    qt = qt_ref[...]
    ch = lax.broadcasted_iota(jnp.int32, (pair_w, 1), 0)
    key_r = lax.broadcasted_iota(jnp.int32, (blk, blk), 0)
    qry_c = lax.broadcasted_iota(jnp.int32, (blk, blk), 1)
    causal = key_r <= qry_c
    scale = HEAD_DIM ** -0.5 * LOG2_E

    qhs = [jnp.where((ch >= h * HEAD_DIM) & (ch < (h + 1) * HEAD_DIM), qt, 0.0) for h in range(2)]
    qs = jnp.concatenate([(qh * scale).astype(BF16) for qh in qhs], axis=1)
    st = jnp.dot(k_ref[0:nk, :], qs, preferred_element_type=F32)
    causal2 = jnp.concatenate([causal, causal], axis=1)
    pieces = []
    if i > MOBA_TOPK:
        sels = [_topk_rows(_dot_split(means_ref[...], qh, _NN), i, MOBA_TOPK) for qh in qhs]
        for j in range(i):
            sel_j = jnp.concatenate([jnp.where(s[j], 1.0, 0.0) for s in sels], axis=1)
            pieces.append(jnp.where(sel_j > 0.5, st[j * blk:(j + 1) * blk], NEG_INF))
    else:
        pieces = [st[j * blk:(j + 1) * blk] for j in range(i)]
    pieces.append(jnp.where(causal2, st[i * blk:], NEG_INF))
    sm = jnp.concatenate(pieces, axis=0) if i else pieces[0]
    m = jnp.max(sm, axis=0, keepdims=True)
    p = jnp.exp2(sm - m).astype(BF16)
    vt1 = jnp.concatenate([vt_ref[:, 0:nk], jnp.ones((BF16_SUBLANES, nk), BF16)], axis=0)
    acc = jnp.dot(vt1, p, preferred_element_type=F32)
    out2 = acc[0:pair_w] / acc[pair_w:pair_w + 1]
    out_t = jnp.where(ch < HEAD_DIM, out2[:, 0:blk], out2[:, blk:])
    o_ref[...] = out_t.T.astype(o_ref.dtype)


def _moba_prompt_kernel(qt_ref, k_ref, vt_ref, means_ref, o_ref, *, nblk):
    i = pl.program_id(2)
    for c in range(nblk):
        pl.when(i == c)(functools.partial(_moba_prompt_block, c, qt_ref, k_ref, vt_ref, means_ref, o_ref))


def _moba_prompt(qt, kbf, vtb, means, n_batch, seq):
    assert seq % MOBA_BLOCK == 0
    nblk = seq // MOBA_BLOCK
    pair_w = 2 * HEAD_DIM
    n_pairs = BRANCH_W // pair_w
    out = pl.pallas_call(
        functools.partial(_moba_prompt_kernel, nblk=nblk),
        grid=(n_batch, n_pairs, nblk),
        in_specs=[pl.BlockSpec((None, pair_w, MOBA_BLOCK), lambda b, hp, i: (b, hp, i)),
                  pl.BlockSpec((None, seq, pair_w), lambda b, hp, i: (b, 0, hp)),
                  pl.BlockSpec((None, pair_w, seq), lambda b, hp, i: (b, hp, 0)),
                  pl.BlockSpec((None, nblk, pair_w), lambda b, hp, i: (b, 0, hp))],
        out_specs=pl.BlockSpec((None, MOBA_BLOCK, pair_w), lambda b, hp, i: (b, i, hp)),
        out_shape=jax.ShapeDtypeStruct((n_batch, seq, BRANCH_W), BF16),
        compiler_params=_cparams(3),
        name="moba_prompt",
    )(qt, kbf.reshape(n_batch, seq, BRANCH_W), vtb, means)
    return out.reshape(n_batch * seq, BRANCH_W)


CHUNK_PAGES = 8
DMA_SLOTS = 4


def _moba_sample_kernel(pt_ref, q_ref, kn_ref, vn_ref, ck_ref, cv_ref, o_ref,
                        buf, sem, s_sc, *, layer, n_batch, n_pages, sq):
    b = pl.program_id(0)
    ppb = MOBA_BLOCK // PAGE_SIZE
    nblk = n_pages // ppb
    n_chunks = n_pages // CHUNK_PAGES
    bpc = CHUNK_PAGES // ppb
    nrow = N_HEADS * sq

    def chunk_copies(bb, c, slot):
        src = ck_ref if c < n_chunks else cv_ref
        cc = c % n_chunks
        return [pltpu.make_async_copy(src.at[layer, pt_ref[bb, cc * CHUNK_PAGES + p]],
                                      buf.at[slot, p], sem.at[slot])
                for p in range(CHUNK_PAGES)]

    def start_chunk(bb, c):
        for cp in chunk_copies(bb, c, c % DMA_SLOTS):
            cp.start()

    def wait_chunk(bb, c):
        for cp in chunk_copies(bb, c, c % DMA_SLOTS):
            cp.wait()

    def prefetch(c):
        nxt = c + DMA_SLOTS - 1
        if nxt < 2 * n_chunks:
            start_chunk(b, nxt)
        else:
            @pl.when(b + 1 < n_batch)
            def _():
                start_chunk(b + 1, nxt - 2 * n_chunks)

    @pl.when(b == 0)
    def _():
        for c in range(DMA_SLOTS - 1):
            start_chunk(b, c)

    qb = q_ref[...]
    qrows = jnp.concatenate([qb] * N_HEADS, axis=0)
    rh = lax.broadcasted_iota(jnp.int32, (nrow, BRANCH_W), 0) >> (sq.bit_length() - 1)
    chh = lax.broadcasted_iota(jnp.int32, (nrow, BRANCH_W), 1) >> (HEAD_DIM.bit_length() - 1)
    head_mask = rh == chh
    qrows = jnp.where(head_mask, qrows, 0.0)
    qs = (qrows * (HEAD_DIM ** -0.5)).astype(BF16)

    def block_of(slot, n):
        return jnp.concatenate([buf[slot, pg] for pg in range(n * ppb, (n + 1) * ppb)], axis=1)

    lane_i = lax.broadcasted_iota(jnp.int32, (nrow, LANES), 1)
    gs = jnp.zeros((nrow, LANES), F32)
    for c in range(n_chunks):
        prefetch(c)
        wait_chunk(b, c)
        for n in range(bpc):
            blk_id = c * bpc + n
            s = jnp.dot(qs, block_of(c % DMA_SLOTS, n).astype(BF16), preferred_element_type=F32)
            s_sc[blk_id] = s
            gs = jnp.where(lane_i == blk_id, jnp.sum(s, axis=1, keepdims=True), gs)

    lane = lane_i.astype(F32)
    sel_f = jnp.zeros(gs.shape, F32)
    work = jnp.where(lane_i < nblk, gs, -jnp.inf)
    for _ in range(min(MOBA_TOPK, nblk)):
        mx = jnp.max(work, axis=1, keepdims=True)
        first = jnp.min(jnp.where(work == mx, lane, float(nblk)), axis=1, keepdims=True)
        pick = lane == first
        sel_f = jnp.where(pick, 1.0, sel_f)
        work = jnp.where(pick, -jnp.inf, work)

    s_own = lax.dot_general(qs, kn_ref[...].astype(BF16), _NT, preferred_element_type=F32)
    qi = lax.broadcasted_iota(jnp.int32, (nrow, sq), 0) & (sq - 1)
    kk = lax.broadcasted_iota(jnp.int32, (nrow, sq), 1)
    s_own = jnp.where(kk <= qi, s_own, NEG_INF)
    def masked(blk_id):
        return jnp.where(sel_f[:, blk_id:blk_id + 1] > 0.5, s_sc[blk_id], NEG_INF)

    m_run = masked(0)
    for n in range(1, nblk):
        m_run = jnp.maximum(m_run, masked(n))
    m = jnp.maximum(jnp.max(s_own, axis=1, keepdims=True), jnp.max(m_run, axis=1, keepdims=True))
    p_own = jnp.exp(s_own - m)
    acc = jnp.dot(p_own.astype(BF16), vn_ref[...].astype(BF16), preferred_element_type=F32)

    l_run = jnp.zeros((nrow, MOBA_BLOCK), F32)
    for c in range(n_chunks, 2 * n_chunks):
        prefetch(c)
        wait_chunk(b, c)
        for n in range(bpc):
            p = jnp.exp(masked((c - n_chunks) * bpc + n) - m)
            l_run = l_run + p
            acc = acc + lax.dot_general(p.astype(BF16), block_of(c % DMA_SLOTS, n).astype(BF16), _NT,
                                        preferred_element_type=F32)
    l = jnp.sum(p_own, axis=1, keepdims=True) + jnp.sum(l_run, axis=1, keepdims=True)

    full = jnp.where(head_mask, acc / l, 0.0)
    out = full[0:sq]
    for h in range(1, N_HEADS):
        out = out + full[h * sq:(h + 1) * sq]
    o_ref[...] = out.astype(o_ref.dtype)


def _moba_sample(q, k_new, v_new, cache_k, cache_v, page_table, layer, n_batch, sq):
    n_pages = page_table.shape[1]
    assert (n_pages * PAGE_SIZE) % MOBA_BLOCK == 0, "own block must hold only the new rows"
    assert n_pages % CHUNK_PAGES == 0 and (2 * n_pages // CHUNK_PAGES) % DMA_SLOTS == 0
    nblk = n_pages * PAGE_SIZE // MOBA_BLOCK
    assert nblk <= LANES, "block scores are kept one lane per block"
    depth, n_pool = cache_k.shape[0], cache_k.shape[1]
    ck = cache_k.transpose(0, 1, 3, 4, 2).reshape(depth, n_pool, BRANCH_W, PAGE_SIZE)
    cv = cache_v.transpose(0, 1, 3, 4, 2).reshape(depth, n_pool, BRANCH_W, PAGE_SIZE)
    q3, k3, v3 = (t.reshape(n_batch, sq, BRANCH_W) for t in (q, k_new, v_new))
    row = pl.BlockSpec((None, sq, BRANCH_W), lambda b, pt: (b, 0, 0))
    hbm = pl.BlockSpec(memory_space=pl.ANY)
    out = pl.pallas_call(
        functools.partial(_moba_sample_kernel, layer=layer, n_batch=n_batch, n_pages=n_pages, sq=sq),
        grid_spec=pltpu.PrefetchScalarGridSpec(
            num_scalar_prefetch=1,
            grid=(n_batch,),
            in_specs=[row, row, row, hbm, hbm],
            out_specs=row,
            scratch_shapes=[pltpu.VMEM((DMA_SLOTS, CHUNK_PAGES, BRANCH_W, PAGE_SIZE), F32),
                            pltpu.SemaphoreType.DMA((DMA_SLOTS,)),
                            pltpu.VMEM((nblk, N_HEADS * sq, MOBA_BLOCK), F32)]),
        out_shape=jax.ShapeDtypeStruct((n_batch, sq, BRANCH_W), BF16),
        compiler_params=_cparams(1),
        name="moba_sample",
    )(page_table, q3, k3, v3, ck, cv)
    return out.reshape(n_batch * sq, BRANCH_W)


def _merge_kernel(x_ref, a_ref, b_ref, c_ref, gpre_ref, gpost_ref, wg_ref, proj_ref, wo_ref, o_ref):
    x = x_ref[...]
    u = _rms(x, gpre_ref[...]).astype(BF16)
    gates = jax.nn.sigmoid(jnp.dot(u, wg_ref[...], preferred_element_type=F32))
    merged = None
    for n, br in enumerate((a_ref, b_ref, c_ref)):
        t = gates[:, n * D_MODEL:(n + 1) * D_MODEL] * jnp.dot(br[...], proj_ref[n], preferred_element_type=F32)
        merged = t if merged is None else merged + t
    m = jnp.dot(merged.astype(BF16), wo_ref[...], preferred_element_type=F32)
    o_ref[...] = x + _rms(m, gpost_ref[...])


def _merge(x, a, b, c, gpre, gpost, wg, proj, wo, tm):
    n = x.shape[0]
    row = lambda w: pl.BlockSpec((tm, w), lambda i: (i, 0))
    consts = (gpre, gpost, wg, proj, wo)
    return pl.pallas_call(
        _merge_kernel,
        grid=(n // tm,),
        in_specs=[row(D_MODEL)] + [row(BRANCH_W)] * 3 + [_const_spec(t.shape) for t in consts],
        out_specs=row(D_MODEL),
        out_shape=jax.ShapeDtypeStruct((n, D_MODEL), F32),
        compiler_params=_cparams(1),
        name="merge",
    )(x, a, b, c, *consts)


def _ffn_dense_kernel(x_ref, gpre_ref, gpost_ref, wg_ref, wu_ref, wd_ref, o_ref, u_sc, acc_sc):
    f = pl.program_id(1)

    @pl.when(f == 0)
    def _():
        u_sc[...] = _rms(x_ref[...], gpre_ref[...]).astype(BF16)
        acc_sc[...] = jnp.zeros(acc_sc.shape, F32)

    u = u_sc[...]
    hg = jnp.dot(u, wg_ref[...], preferred_element_type=F32)
    hu = jnp.dot(u, wu_ref[...], preferred_element_type=F32)
    acc_sc[...] += jnp.dot((jax.nn.silu(hg) * hu).astype(BF16), wd_ref[...], preferred_element_type=F32)

    @pl.when(f == pl.num_programs(1) - 1)
    def _():
        o_ref[...] = x_ref[...] + _rms(acc_sc[...], gpost_ref[...])


def _ffn_dense(x, gpre, gpost, wg, wu, wd, tm, fc):
    n = x.shape[0]
    dff = wg.shape[1]
    row = pl.BlockSpec((tm, D_MODEL), lambda i, f: (i, 0))
    mode = dict(pipeline_mode=pl.Buffered(1)) if fc == dff else {}
    return pl.pallas_call(
        _ffn_dense_kernel,
        grid=(n // tm, dff // fc),
        in_specs=[row, _const_spec(gpre.shape), _const_spec(gpost.shape),
                  pl.BlockSpec((D_MODEL, fc), lambda i, f: (0, f), **mode),
                  pl.BlockSpec((D_MODEL, fc), lambda i, f: (0, f), **mode),
                  pl.BlockSpec((fc, D_MODEL), lambda i, f: (f, 0), **mode)],
        out_specs=row,
        out_shape=jax.ShapeDtypeStruct((n, D_MODEL), F32),
        scratch_shapes=[pltpu.VMEM((tm, D_MODEL), BF16), pltpu.VMEM((tm, D_MODEL), F32)],
        compiler_params=_cparams(2),
        name="ffn_dense",
    )(x, gpre, gpost, wg, wu, wd)


def _router_gate(u, router_ref):
    logits = _dot_split(u, router_ref[...], _NN)
    lane = lax.broadcasted_iota(jnp.int32, logits.shape, 1)
    logits = jnp.where(lane < N_EXPERTS, logits, -jnp.inf)
    m1 = jnp.max(logits, axis=1, keepdims=True)
    i1 = jnp.min(jnp.where(logits == m1, lane, LANES), axis=1, keepdims=True)
    rest = jnp.where(lane == i1, -jnp.inf, logits)
    m2 = jnp.max(rest, axis=1, keepdims=True)
    i2 = jnp.min(jnp.where(rest == m2, lane, LANES), axis=1, keepdims=True)
    e2 = jnp.exp(m2 - m1)
    w1 = 1.0 / (1.0 + e2)
    w2 = e2 / (1.0 + e2)
    gate = jnp.where(lane == i1, w1, 0.0) + jnp.where(lane == i2, w2, 0.0)
    chosen = jnp.where((lane == i1) | (lane == i2), 1.0, 0.0)
    return gate, chosen


MOE_SLAB = 256


def _moe_kernel(x_ref, gpre_ref, gpost_ref, router_ref, wg_ref, wu_ref, wd_ref, o_ref,
                u_sc, gate_sc, rank_sc, rankt_sc, cnt_sc, xs_sc, y_sc, *, tm, ck, sck):
    e = pl.program_id(1)
    f = pl.program_id(2)
    nf = pl.num_programs(2)

    @pl.when((e == 0) & (f == 0))
    def _():
        u = _rms(x_ref[...], gpre_ref[...])
        u_sc[...] = u.astype(BF16)
        gate, chosen = _router_gate(u, router_ref)
        gate_sc[...] = gate
        earlier = (lax.broadcasted_iota(jnp.int32, (tm, tm), 1) < lax.broadcasted_iota(jnp.int32, (tm, tm), 0))
        rank = jnp.dot(jnp.where(earlier, 1.0, 0.0).astype(BF16), chosen.astype(BF16), preferred_element_type=F32)
        rank = jnp.where(chosen > 0.5, rank, -1.0)
        rank_sc[...] = rank
        rankt_sc[...] = rank.T[0:SUBLANES, :]
        cnt_sc[...] = jnp.sum(chosen, axis=0, keepdims=True)
        o_ref[...] = jnp.zeros(o_ref.shape, F32)
        y_sc[...] = jnp.zeros(y_sc.shape, F32)

    lane1 = lax.broadcasted_iota(jnp.int32, (1, LANES), 1)
    cnt = jnp.sum(jnp.where(lane1 == e, cnt_sc[...], 0.0))
    n_chunks = jnp.ceil(cnt * (1.0 / ck)).astype(jnp.int32)

    @pl.when(f == 0)
    def _():
        rank_row = rankt_sc[pl.ds(e, 1), :]
        slot = lax.broadcasted_iota(jnp.int32, (ck, 1), 0).astype(F32)

        def compact(c, carry):
            base = pl.multiple_of(c * ck, ck)
            onehot = jnp.where(rank_row - (c * ck).astype(F32) == slot, 1.0, 0.0).astype(BF16)
            xs_sc[pl.ds(base, ck), :] = jnp.dot(onehot, u_sc[...], preferred_element_type=F32).astype(BF16)
            return carry

        lax.fori_loop(0, n_chunks, compact, 0)

    def expert(c, carry):
        base = pl.multiple_of(c * ck, ck)
        xs = xs_sc[pl.ds(base, ck), :]
        hg = jnp.dot(xs, wg_ref[...], preferred_element_type=F32)
        hu = jnp.dot(xs, wu_ref[...], preferred_element_type=F32)
        y = jnp.dot((jax.nn.silu(hg) * hu).astype(BF16), wd_ref[...], preferred_element_type=F32)

        @pl.when(f == 0)
        def _():
            y_sc[pl.ds(base, ck), :] = y

        @pl.when(f > 0)
        def _():
            y_sc[pl.ds(base, ck), :] += y

        return carry

    lax.fori_loop(0, n_chunks, expert, 0)

    @pl.when(f == nf - 1)
    def _():
        lane = lax.broadcasted_iota(jnp.int32, (MOE_SLAB, LANES), 1)
        slot = lax.broadcasted_iota(jnp.int32, (1, sck), 1).astype(F32)

        def scatter(c, carry):
            base = pl.multiple_of(c * sck, sck)
            yb = y_sc[pl.ds(base, sck), :].astype(BF16)
            for s0 in range(0, tm, MOE_SLAB):
                rows = slice(s0, s0 + MOE_SLAB)
                rank_col = jnp.sum(jnp.where(lane == e, rank_sc[rows, :], 0.0), axis=1, keepdims=True)
                gate_col = jnp.sum(jnp.where(lane == e, gate_sc[rows, :], 0.0), axis=1, keepdims=True)
                onehot_t = jnp.where(rank_col - (c * sck).astype(F32) == slot, 1.0, 0.0).astype(BF16)
                o_ref[rows, :] += gate_col * jnp.dot(onehot_t, yb, preferred_element_type=F32)
            return carry

        lax.fori_loop(0, (n_chunks * ck + sck - 1) // sck, scatter, 0)

    @pl.when((e == pl.num_programs(1) - 1) & (f == nf - 1))
    def _():
        o_ref[...] = x_ref[...] + _rms(o_ref[...], gpost_ref[...])


def _moe(x, gpre, gpost, router_pad, wg, wu, wd, tm, fc, ck):
    n = x.shape[0]
    n_exp, _, dff = wg.shape
    sck = 2 * ck if 2 * ck <= MXU_DEPTH else ck
    assert n_exp <= SUBLANES and tm % MOE_SLAB == 0 and ck % BF16_SUBLANES == 0
    cap = pl.cdiv(tm, sck) * sck
    row = pl.BlockSpec((tm, D_MODEL), lambda i, e, f: (i, 0))
    return pl.pallas_call(
        functools.partial(_moe_kernel, tm=tm, ck=ck, sck=sck),
        grid=(n // tm, n_exp, dff // fc),
        in_specs=[row, _const_spec(gpre.shape), _const_spec(gpost.shape), _const_spec(router_pad.shape),
                  pl.BlockSpec((None, D_MODEL, fc), lambda i, e, f: (e, 0, f)),
                  pl.BlockSpec((None, D_MODEL, fc), lambda i, e, f: (e, 0, f)),
                  pl.BlockSpec((None, fc, D_MODEL), lambda i, e, f: (e, f, 0))],
        out_specs=row,
        out_shape=jax.ShapeDtypeStruct((n, D_MODEL), F32),
        scratch_shapes=[pltpu.VMEM((tm, D_MODEL), BF16), pltpu.VMEM((tm, LANES), F32),
                        pltpu.VMEM((tm, LANES), F32), pltpu.VMEM((SUBLANES, tm), F32),
                        pltpu.VMEM((1, LANES), F32),
                        pltpu.VMEM((cap, D_MODEL), BF16), pltpu.VMEM((cap, D_MODEL), F32)],
        compiler_params=_cparams(3),
        name="moe",
    )(x, gpre, gpost, router_pad, wg, wu, wd)


def _block_diag_halves(w_a, w_i):
    half_blocks = LRU_BLOCKS // 2

    def bd(w4):
        eye = jnp.eye(half_blocks, dtype=w4.dtype)
        return jnp.einsum('hcd,hg->hcgd', w4, eye).reshape(half_blocks * LRU_BLOCK_W, half_blocks * LRU_BLOCK_W)

    halves = [jnp.concatenate([bd(w_a[i * half_blocks:(i + 1) * half_blocks]),
                               bd(w_i[i * half_blocks:(i + 1) * half_blocks])], axis=1) for i in range(2)]
    return jnp.stack(halves).astype(BF16)


def _row(v):
    return v.reshape(1, -1)


def _pad_hist(h, pad_rows):
    return jnp.pad(h, ((0, 0), (pad_rows - h.shape[1], 0), (0, 0)))


PROMPT_TS = 512
ROW_TM = 512
FFN_FC = 2816
MOE_FC = 1792
MOE_TM = 1024
MOE_CHUNK = 288
MOE_CHUNK_SMALL = 128


def kernel(x_prompt, x_sample, cache_k, cache_v, page_table, state_pool, state_conv, state_h, norm_mix_pre, norm_mix_post, norm_ffn_pre, norm_ffn_post, w_in, pool_w, pool_scale, conv_w, conv_b, lru_wa, lru_ba, lru_wi, lru_bi, lru_lambda, proj_pool, proj_attn, proj_lru, w_o, ffn_w_gate, ffn_w_up, ffn_w_down, router_w, moe_w_gate, moe_w_up, moe_w_down):
    depth = w_in.shape[0]
    bp, sp, _ = x_prompt.shape
    bs, ss, _ = x_sample.shape
    past_len = page_table.shape[1] * PAGE_SIZE
    xp = x_prompt.reshape(bp * sp, D_MODEL)
    xs = x_sample.reshape(bs * ss, D_MODEL)
    tm_s = bs * ss

    outs = {name: [] for name in ("ks", "vs", "pp", "ps", "cp", "cs", "hp", "hs")}
    kv_pages = tuple(jnp.zeros((depth, bp, sp // PAGE_SIZE, BRANCH_W, PAGE_SIZE), F32) for _ in range(2))
    for l in range(depth):
        w_qkv = w_in[l, :, :D_QKV].astype(BF16)
        w_gates = w_in[l, :, D_QKV:].astype(BF16)
        mix_consts = (_row(norm_mix_pre[l]), w_qkv, pool_w[l].astype(BF16), _row(pool_scale[l]),
                      conv_w[l], _row(conv_b[l]), _block_diag_halves(lru_wa[l], lru_wi[l]),
                      _row(lru_ba[l]), _row(lru_bi[l]), _row(lru_lambda[l]))
        proj = jnp.stack([proj_pool[l], proj_attn[l], proj_lru[l]]).astype(BF16)
        merge_consts = (_row(norm_mix_pre[l]), _row(norm_mix_post[l]), w_gates, proj, w_o[l].astype(BF16))

        qt_p, kbf_p, vtb_p, kmean_p, kpages, vpages, a_p, c_p, pt_p, ct_p, h_p = _mixer_in_prompt(
            xp, bp, sp, PROMPT_TS, l, depth, kv_pages, *mix_consts)
        kv_pages = (kpages, vpages)
        b_p = _moba_prompt(qt_p, kbf_p, vtb_p, kmean_p.reshape(bp, sp // MOBA_BLOCK, BRANCH_W), bp, sp)
        xp = _merge(xp, a_p, b_p, c_p, *merge_consts, ROW_TM)

        q_s, k_s, v_s, a_s, c_s, pt_s, ct_s, h_s = _mixer_in_sample(
            xs, bs, ss, past_len, _pad_hist(state_pool[l], POOL_PAD), _pad_hist(state_conv[l], CONV_PAD),
            state_h[l].reshape(bs, 1, BRANCH_W), *mix_consts)
        b_s = _moba_sample(q_s, k_s, v_s, cache_k, cache_v, page_table, l, bs, ss)
        xs = _merge(xs, a_s, b_s, c_s, *merge_consts, tm_s)

        gpre, gpost = _row(norm_ffn_pre[l]), _row(norm_ffn_post[l])
        j = l // 2
        if l % 2 == 0:
            wg, wu, wd = (w[j].astype(BF16) for w in (ffn_w_gate, ffn_w_up, ffn_w_down))
            xp = _ffn_dense(xp, gpre, gpost, wg, wu, wd, ROW_TM, FFN_FC)
            xs = _ffn_dense(xs, gpre, gpost, wg, wu, wd, tm_s, FFN_FC)
        else:
            wg, wu, wd = (w[j].astype(BF16) for w in (moe_w_gate, moe_w_up, moe_w_down))
            router_pad = jnp.pad(router_w[j], ((0, 0), (0, LANES - N_EXPERTS)))
            xp = _moe(xp, gpre, gpost, router_pad, wg, wu, wd, MOE_TM, MOE_FC, MOE_CHUNK)
            xs = _moe(xs, gpre, gpost, router_pad, wg, wu, wd, tm_s, MOE_FC, MOE_CHUNK_SMALL)

        outs["ks"].append(k_s.reshape(bs, ss, N_HEADS, HEAD_DIM))
        outs["vs"].append(v_s.reshape(bs, ss, N_HEADS, HEAD_DIM))
        outs["pp"].append(pt_p[:, POOL_PAD - POOL_HIST:])
        outs["ps"].append(pt_s[:, POOL_PAD - POOL_HIST:])
        outs["cp"].append(ct_p[:, CONV_PAD - (CONV_W - 1):])
        outs["cs"].append(ct_s[:, CONV_PAD - (CONV_W - 1):])
        outs["hp"].append(h_p.reshape(bp, BRANCH_W))
        outs["hs"].append(h_s.reshape(bs, BRANCH_W))

    st = {k: jnp.stack(v) for k, v in outs.items()}
    kp, vp = (t.reshape(depth, bp, sp // PAGE_SIZE, N_HEADS, HEAD_DIM, PAGE_SIZE).transpose(0, 1, 2, 5, 3, 4)
              for t in kv_pages)
    return (xp.reshape(bp, sp, D_MODEL), xs.reshape(bs, ss, D_MODEL),
            kp, vp, st["ks"], st["vs"], st["pp"], st["ps"], st["cp"], st["cs"], st["hp"], st["hs"])
```

```python
import functools

import jax
import jax.numpy as jnp
from jax import lax
from jax.experimental import pallas as pl
from jax.experimental.pallas import tpu as pltpu

F32 = jnp.float32
BF16 = jnp.bfloat16

D_MODEL = 1024
BRANCH_W = 512
POOL_WINDOWS = (2, 4, 8, 16)
POOL_GROUP_W = BRANCH_W // len(POOL_WINDOWS)
POOL_HIST = max(POOL_WINDOWS) - 1
POOL_PAD = 16
N_HEADS = 8
HEAD_DIM = BRANCH_W // N_HEADS
MOBA_BLOCK = 256
MOBA_TOPK = 3
PAGE_SIZE = 128
LRU_BLOCKS = 8
LRU_BLOCK_W = BRANCH_W // LRU_BLOCKS
CONV_W = 4
CONV_PAD = 8
LRU_C = 8.0
N_EXPERTS = 8
TOP_K = 2
EPS = 1e-6
NEG_INF = -1e30
LOG2_E = 1.4426950408889634
D_QKV = 6 * BRANCH_W
SUBLANES = 8
BF16_SUBLANES = 16
MXU_DEPTH = 256
LANES = 128
VMEM_LIMIT = 56 * 1024 * 1024


def _cparams(n_axes):
    return pltpu.CompilerParams(
        dimension_semantics=("arbitrary",) * n_axes, vmem_limit_bytes=VMEM_LIMIT)


def _rms(x, g):
    return x * lax.rsqrt(jnp.mean(x * x, axis=-1, keepdims=True) + EPS) * g


def _split_bf16(x):
    hi = x.astype(BF16)
    lo = (x - hi.astype(F32)).astype(BF16)
    return hi, lo


def _dot_split(a, b, dims):
    a_hi, a_lo = _split_bf16(a)
    b_hi, b_lo = _split_bf16(b)
    d = functools.partial(lax.dot_general, dimension_numbers=dims, preferred_element_type=F32)
    return d(a_hi, b_hi) + (d(a_hi, b_lo) + d(a_lo, b_hi))


_NN = (((1,), (0,)), ((), ()))
_NT = (((1,), (1,)), ((), ()))


def _sigmoid(x):
    return 0.5 * jnp.tanh(0.5 * x) + 0.5


def _softplus(x):
    return jnp.maximum(x, 0.0) + jnp.log1p(jnp.exp(-jnp.abs(x)))


def _lru_coeffs(xc, wbd_ref, ba, bi, lam, first_row_is_start):
    xcb = xc.astype(BF16)
    half = BRANCH_W // 2
    g0 = jnp.dot(xcb[:, :half], wbd_ref[0], preferred_element_type=F32)
    g1 = jnp.dot(xcb[:, half:], wbd_ref[1], preferred_element_type=F32)
    pre_a = jnp.concatenate([g0[:, :half], g1[:, :half]], axis=1)
    pre_i = jnp.concatenate([g0[:, half:], g1[:, half:]], axis=1)
    r = _sigmoid(pre_a + ba)
    i = _sigmoid(pre_i + bi)
    log_a = (-LRU_C) * r * _softplus(-lam)
    a = jnp.exp(log_a)
    t = jnp.tanh(log_a)
    mult = jnp.sqrt(-2.0 * t) * lax.rsqrt(1.0 - t)
    if first_row_is_start is not None:
        mult = jnp.where(first_row_is_start, 1.0, mult)
    return a, mult * i * xc


def _scan8(a, u):
    rows, width = a.shape
    a = a.reshape(rows // SUBLANES, SUBLANES, width)
    u = u.reshape(rows // SUBLANES, SUBLANES, width)
    r8 = lax.broadcasted_iota(jnp.int32, a.shape, 1)
    for d in (1, 2, 4):
        a_s = pltpu.roll(a, d, axis=1)
        u_s = pltpu.roll(u, d, axis=1)
        m = r8 >= d
        u = jnp.where(m, u + a * u_s, u)
        a = jnp.where(m, a * a_s, a)
    return a.reshape(rows, width), u.reshape(rows, width)


def _pool_mix(win_groups, xp, cnt_groups, poolw_ref, pscale):
    outs = []
    for g in range(len(POOL_WINDOWS)):
        c0 = g * POOL_GROUP_W
        pooled = win_groups[g] / cnt_groups[g] - xp[:, c0:c0 + POOL_GROUP_W]
        outs.append(jnp.dot(pooled.astype(BF16), poolw_ref[g], preferred_element_type=F32))
    return jnp.concatenate(outs, axis=1) * pscale


def _mixer_in_prompt_kernel(x_ref, gpre_ref, win_ref, poolw_ref, pscale_ref, convw_ref, convb_ref,
                            wbd_ref, ba_ref, bi_ref, lam_ref, *rest, ts):
    (qt_ref, kbf_ref, vtb_ref, kmean_ref, kpage_ref, vpage_ref, a_ref, c_ref,
     ptail_ref, ctail_ref, hlast_ref, pbuf, cbuf, hcar) = rest[-14:]
    s = pl.program_id(1)

    @pl.when(s == 0)
    def _():
        pbuf[0:POOL_PAD, :] = jnp.zeros((POOL_PAD, BRANCH_W), F32)
        cbuf[0:CONV_PAD, :] = jnp.zeros((CONV_PAD, BRANCH_W), F32)
        hcar[...] = jnp.zeros((1, BRANCH_W), F32)

    u = _rms(x_ref[...], gpre_ref[...])
    z = jnp.dot(u.astype(BF16), win_ref[...], preferred_element_type=F32)
    bw = BRANCH_W
    xp = z[:, 0:bw]
    k = z[:, 2 * bw:3 * bw]
    xl = z[:, 4 * bw:5 * bw]
    gl = z[:, 5 * bw:6 * bw]

    qt_ref[...] = z[:, bw:2 * bw].T
    kbf_ref[...] = k.astype(BF16)
    kt = k.T
    vt = z[:, 3 * bw:4 * bw].T
    vtb_ref[...] = vt.astype(BF16)
    for n in range(ts // MOBA_BLOCK):
        kmean_ref[n:n + 1, :] = jnp.sum(k[n * MOBA_BLOCK:(n + 1) * MOBA_BLOCK], axis=0, keepdims=True) * (1.0 / MOBA_BLOCK)
    for p in range(ts // PAGE_SIZE):
        kpage_ref[p] = kt[:, p * PAGE_SIZE:(p + 1) * PAGE_SIZE]
        vpage_ref[p] = vt[:, p * PAGE_SIZE:(p + 1) * PAGE_SIZE]

    pos = s * ts + lax.broadcasted_iota(jnp.int32, (ts, 1), 0)

    pbuf[POOL_PAD:POOL_PAD + ts, :] = xp
    wins, cnts = [], []
    for g, w in enumerate(POOL_WINDOWS):
        c0 = g * POOL_GROUP_W
        win = pbuf[:, c0:c0 + POOL_GROUP_W]
        d = 1
        while d < w:
            win = win + pltpu.roll(win, d, axis=0)
            d *= 2
        wins.append(win[POOL_PAD:])
        cnts.append(jnp.minimum(pos + 1, w).astype(F32))
    a_ref[...] = _pool_mix(wins, xp, cnts, poolw_ref, pscale_ref[...]).astype(a_ref.dtype)
    tail = pbuf[ts:ts + POOL_PAD, :]
    pbuf[0:POOL_PAD, :] = tail
    ptail_ref[...] = tail

    cbuf[CONV_PAD:CONV_PAD + ts, :] = xl
    xc = convb_ref[...] + xl * convw_ref[CONV_W - 1:CONV_W, :]
    for j in range(CONV_W - 1):
        off = CONV_PAD - (CONV_W - 1) + j
        xc = xc + cbuf[off:off + ts, :] * convw_ref[j:j + 1, :]
    ctail = cbuf[ts:ts + CONV_PAD, :]
    cbuf[0:CONV_PAD, :] = ctail
    ctail_ref[...] = ctail

    a, uu = _lru_coeffs(xc, wbd_ref, ba_ref[...], bi_ref[...], lam_ref[...], pos == 0)
    a, uu = _scan8(a, uu)
    carry = jnp.broadcast_to(hcar[...], (SUBLANES, BRANCH_W))
    hs = []
    for g in range(ts // SUBLANES):
        hg = a[g * SUBLANES:(g + 1) * SUBLANES] * carry + uu[g * SUBLANES:(g + 1) * SUBLANES]
        hs.append(hg)
        carry = jnp.broadcast_to(hg[SUBLANES - 1:SUBLANES, :], (SUBLANES, BRANCH_W))
    h = jnp.concatenate(hs, axis=0)
    hcar[...] = carry[0:1, :]
    hlast_ref[...] = carry[0:1, :]
    c_ref[...] = (h * jax.nn.gelu(gl)).astype(c_ref.dtype)


def _const_spec(shape):
    nd = len(shape)
    return pl.BlockSpec(shape, lambda *_: (0,) * nd)


def _mixer_in_prompt(x, n_batch, seq, ts, layer, depth, kv_pages, gpre, win, poolw, pscale, convw, convb,
                     wbd, ba, bi, lam):
    assert ts % MOBA_BLOCK == 0 and seq % ts == 0
    n = n_batch * seq
    nt = seq // ts
    ppt = ts // PAGE_SIZE
    bpt = ts // MOBA_BLOCK
    row = lambda w: pl.BlockSpec((ts, w), lambda b, s: (b * nt + s, 0))
    chan = pl.BlockSpec((None, BRANCH_W, ts), lambda b, s: (b, 0, s))
    per_b = lambda r: pl.BlockSpec((None, r, BRANCH_W), lambda b, s: (b, 0, 0))
    pages = pl.BlockSpec((None, None, ppt, BRANCH_W, PAGE_SIZE), lambda b, s: (layer, b, s, 0, 0))
    page_shape = jax.ShapeDtypeStruct((depth, n_batch, seq // PAGE_SIZE, BRANCH_W, PAGE_SIZE), F32)
    consts = (gpre, win, poolw, pscale, convw, convb, wbd, ba, bi, lam)
    extra = tuple(kv_pages)
    assert all(t.shape == page_shape.shape and t.dtype == page_shape.dtype for t in extra)
    n_in = 1 + len(consts)
    return pl.pallas_call(
        functools.partial(_mixer_in_prompt_kernel, ts=ts),
        grid=(n_batch, nt),
        in_specs=[row(D_MODEL)] + [_const_spec(c.shape) for c in consts]
        + [pl.BlockSpec(memory_space=pl.ANY)] * len(extra),
        out_specs=[chan, row(BRANCH_W), chan,
                   pl.BlockSpec((None, None, bpt, BRANCH_W), lambda b, s: (b, s, 0, 0)),
                   pages, pages, row(BRANCH_W), row(BRANCH_W),
                   per_b(POOL_PAD), per_b(CONV_PAD), per_b(1)],
        out_shape=[jax.ShapeDtypeStruct((n_batch, BRANCH_W, seq), F32),
                   jax.ShapeDtypeStruct((n, BRANCH_W), BF16),
                   jax.ShapeDtypeStruct((n_batch, BRANCH_W, seq), BF16),
                   jax.ShapeDtypeStruct((n_batch, nt, bpt, BRANCH_W), F32),
                   page_shape, page_shape,
                   jax.ShapeDtypeStruct((n, BRANCH_W), BF16),
                   jax.ShapeDtypeStruct((n, BRANCH_W), BF16),
                   jax.ShapeDtypeStruct((n_batch, POOL_PAD, BRANCH_W), F32),
                   jax.ShapeDtypeStruct((n_batch, CONV_PAD, BRANCH_W), F32),
                   jax.ShapeDtypeStruct((n_batch, 1, BRANCH_W), F32)],
        scratch_shapes=[pltpu.VMEM((POOL_PAD + ts, BRANCH_W), F32),
                        pltpu.VMEM((CONV_PAD + ts, BRANCH_W), F32),
                        pltpu.VMEM((1, BRANCH_W), F32)],
        input_output_aliases={n_in + i: 4 + i for i in range(len(extra))},
        compiler_params=_cparams(2),
        name="mixer_in_prompt",
    )(x, *consts, *extra)


def _mixer_in_sample_kernel(x_ref, phist_ref, chist_ref, h0_ref, gpre_ref, win_ref, poolw_ref,
                            pscale_ref, convw_ref, convb_ref, wbd_ref, ba_ref, bi_ref, lam_ref,
                            q_ref, k_ref, v_ref, a_ref, c_ref, ptail_ref, ctail_ref, hlast_ref,
                            pbuf, cbuf, *, n_batch, sq, start):
    rows = n_batch * sq
    u = _rms(x_ref[...], gpre_ref[...])
    z = jnp.dot(u.astype(BF16), win_ref[...], preferred_element_type=F32)
    bw = BRANCH_W
    xp = z[:, 0:bw]
    q_ref[...] = z[:, bw:2 * bw]
    k_ref[...] = z[:, 2 * bw:3 * bw]
    v_ref[...] = z[:, 3 * bw:4 * bw]
    xl = z[:, 4 * bw:5 * bw]
    gl = z[:, 5 * bw:6 * bw]

    pos = start + (lax.broadcasted_iota(jnp.int32, (rows, 1), 0) & (sq - 1))

    pbuf[:, 0:POOL_PAD, :] = phist_ref[...]
    pbuf[:, POOL_PAD:POOL_PAD + sq, :] = xp.reshape(n_batch, sq, bw)
    wins, cnts = [], []
    for g, w in enumerate(POOL_WINDOWS):
        c0 = g * POOL_GROUP_W
        win = pbuf[:, POOL_PAD:POOL_PAD + sq, c0:c0 + POOL_GROUP_W]
        for j in range(1, w):
            win = win + pbuf[:, POOL_PAD - j:POOL_PAD - j + sq, c0:c0 + POOL_GROUP_W]
        wins.append(win.reshape(rows, POOL_GROUP_W))
        cnts.append(jnp.minimum(pos + 1, w).astype(F32))
    a_ref[...] = _pool_mix(wins, xp, cnts, poolw_ref, pscale_ref[...]).astype(a_ref.dtype)
    ptail_ref[...] = pbuf[:, sq:sq + POOL_PAD, :]

    cbuf[:, 0:CONV_PAD, :] = chist_ref[...]
    cbuf[:, CONV_PAD:CONV_PAD + sq, :] = xl.reshape(n_batch, sq, bw)
    xc = xl.reshape(n_batch, sq, bw) * convw_ref[CONV_W - 1:CONV_W, :]
    for j in range(CONV_W - 1):
        off = CONV_PAD - (CONV_W - 1) + j
        xc = xc + cbuf[:, off:off + sq, :] * convw_ref[j:j + 1, :]
    xc = xc.reshape(rows, bw) + convb_ref[...]
    ctail_ref[...] = cbuf[:, sq:sq + CONV_PAD, :]

    a, uu = _lru_coeffs(xc, wbd_ref, ba_ref[...], bi_ref[...], lam_ref[...],
                        (pos == 0) if start == 0 else None)
    a, uu = _scan8(a, uu)
    h = a.reshape(n_batch, sq, bw) * h0_ref[...] + uu.reshape(n_batch, sq, bw)
    hlast_ref[...] = h[:, sq - 1:sq, :]
    c_ref[...] = (h.reshape(rows, bw) * jax.nn.gelu(gl)).astype(c_ref.dtype)


def _mixer_in_sample(x, n_batch, sq, start, phist, chist, h0, gpre, win, poolw, pscale, convw, convb,
                     wbd, ba, bi, lam):
    assert sq == SUBLANES, "sample sequences must fill exactly one sublane group"
    rows = n_batch * sq
    args = (x, phist, chist, h0, gpre, win, poolw, pscale, convw, convb, wbd, ba, bi, lam)
    return pl.pallas_call(
        functools.partial(_mixer_in_sample_kernel, n_batch=n_batch, sq=sq, start=start),
        grid=(1,),
        in_specs=[_const_spec(a.shape) for a in args],
        out_specs=[_const_spec((rows, BRANCH_W))] * 5
        + [_const_spec((n_batch, POOL_PAD, BRANCH_W)), _const_spec((n_batch, CONV_PAD, BRANCH_W)),
           _const_spec((n_batch, 1, BRANCH_W))],
        out_shape=[jax.ShapeDtypeStruct((rows, BRANCH_W), F32)] * 3
        + [jax.ShapeDtypeStruct((rows, BRANCH_W), BF16)] * 2
        + [jax.ShapeDtypeStruct((n_batch, POOL_PAD, BRANCH_W), F32),
           jax.ShapeDtypeStruct((n_batch, CONV_PAD, BRANCH_W), F32),
           jax.ShapeDtypeStruct((n_batch, 1, BRANCH_W), F32)],
        scratch_shapes=[pltpu.VMEM((n_batch, POOL_PAD + sq, BRANCH_W), F32),
                        pltpu.VMEM((n_batch, CONV_PAD + sq, BRANCH_W), F32)],
        compiler_params=_cparams(1),
        name="mixer_in_sample",
    )(*args)


def _topk_rows(g, n_valid, k):
    ridx = lax.broadcasted_iota(jnp.int32, g.shape, 0)
    valid = ridx < n_valid
    rows = []
    for j in range(n_valid):
        gj = g[j:j + 1, :]
        beats = jnp.where(ridx < j, jnp.where(g >= gj, 1.0, 0.0), jnp.where(g > gj, 1.0, 0.0))
        cnt = jnp.sum(jnp.where(valid, beats, 0.0), axis=0, keepdims=True)
        rows.append(cnt < float(k))
    return rows


def _moba_prompt_block(i, qt_ref, k_ref, vt_ref, means_ref, o_ref):
    blk = MOBA_BLOCK
    pair_w = 2 * HEAD_DIM
    nk = (i + 1) * blk
    qt = qt_ref[...]
    ch = lax.broadcasted_iota(jnp.int32, (pair_w, 1), 0)
    key_r = lax.broadcasted_iota(jnp.int32, (blk, blk), 0)
    qry_c = lax.broadcasted_iota(jnp.int32, (blk, blk), 1)
    causal = key_r <= qry_c
    scale = HEAD_DIM ** -0.5 * LOG2_E

    qhs = [jnp.where((ch >= h * HEAD_DIM) & (ch < (h + 1) * HEAD_DIM), qt, 0.0) for h in range(2)]
    qs = jnp.concatenate([(qh * scale).astype(BF16) for qh in qhs], axis=1)
    st = jnp.dot(k_ref[0:nk, :], qs, preferred_element_type=F32)
    causal2 = jnp.concatenate([causal, causal], axis=1)
    pieces = []
    if i > MOBA_TOPK:
        sels = [_topk_rows(_dot_split(means_ref[...], qh, _NN), i, MOBA_TOPK) for qh in qhs]
        for j in range(i):
            sel_j = jnp.concatenate([jnp.where(s[j], 1.0, 0.0) for s in sels], axis=1)
            pieces.append(jnp.where(sel_j > 0.5, st[j * blk:(j + 1) * blk], NEG_INF))
    else:
        pieces = [st[j * blk:(j + 1) * blk] for j in range(i)]
    pieces.append(jnp.where(causal2, st[i * blk:], NEG_INF))
    sm = jnp.concatenate(pieces, axis=0) if i else pieces[0]
    m = jnp.max(sm, axis=0, keepdims=True)
    p = jnp.exp2(sm - m).astype(BF16)
    vt1 = jnp.concatenate([vt_ref[:, 0:nk], jnp.ones((BF16_SUBLANES, nk), BF16)], axis=0)
    acc = jnp.dot(vt1, p, preferred_element_type=F32)
    out2 = acc[0:pair_w] / acc[pair_w:pair_w + 1]
    out_t = jnp.where(ch < HEAD_DIM, out2[:, 0:blk], out2[:, blk:])
    o_ref[...] = out_t.T.astype(o_ref.dtype)


def _moba_prompt_kernel(qt_ref, k_ref, vt_ref, means_ref, o_ref, *, nblk):
    i = pl.program_id(2)
    for c in range(nblk):
        pl.when(i == c)(functools.partial(_moba_prompt_block, c, qt_ref, k_ref, vt_ref, means_ref, o_ref))


def _moba_prompt(qt, kbf, vtb, means, n_batch, seq):
    assert seq % MOBA_BLOCK == 0
    nblk = seq // MOBA_BLOCK
    pair_w = 2 * HEAD_DIM
    n_pairs = BRANCH_W // pair_w
    out = pl.pallas_call(
        functools.partial(_moba_prompt_kernel, nblk=nblk),
        grid=(n_batch, n_pairs, nblk),
        in_specs=[pl.BlockSpec((None, pair_w, MOBA_BLOCK), lambda b, hp, i: (b, hp, i)),
                  pl.BlockSpec((None, seq, pair_w), lambda b, hp, i: (b, 0, hp)),
                  pl.BlockSpec((None, pair_w, seq), lambda b, hp, i: (b, hp, 0)),
                  pl.BlockSpec((None, nblk, pair_w), lambda b, hp, i: (b, 0, hp))],
        out_specs=pl.BlockSpec((None, MOBA_BLOCK, pair_w), lambda b, hp, i: (b, i, hp)),
        out_shape=jax.ShapeDtypeStruct((n_batch, seq, BRANCH_W), BF16),
        compiler_params=_cparams(3),
        name="moba_prompt",
    )(qt, kbf.reshape(n_batch, seq, BRANCH_W), vtb, means)
    return out.reshape(n_batch * seq, BRANCH_W)


CHUNK_PAGES = 8
DMA_SLOTS = 4


def _moba_sample_kernel(pt_ref, q_ref, kn_ref, vn_ref, ck_ref, cv_ref, o_ref,
                        buf, sem, s_sc, *, layer, n_batch, n_pages, sq):
    b = pl.program_id(0)
    ppb = MOBA_BLOCK // PAGE_SIZE
    nblk = n_pages // ppb
    n_chunks = n_pages // CHUNK_PAGES
    bpc = CHUNK_PAGES // ppb
    nrow = N_HEADS * sq

    def chunk_copies(bb, c, slot):
        src = ck_ref if c < n_chunks else cv_ref
        cc = c % n_chunks
        return [pltpu.make_async_copy(src.at[layer, pt_ref[bb, cc * CHUNK_PAGES + p]],
                                      buf.at[slot, p], sem.at[slot])
                for p in range(CHUNK_PAGES)]

    def start_chunk(bb, c):
        for cp in chunk_copies(bb, c, c % DMA_SLOTS):
            cp.start()

    def wait_chunk(bb, c):
        for cp in chunk_copies(bb, c, c % DMA_SLOTS):
            cp.wait()

    def prefetch(c):
        nxt = c + DMA_SLOTS - 1
        if nxt < 2 * n_chunks:
            start_chunk(b, nxt)
        else:
            @pl.when(b + 1 < n_batch)
            def _():
                start_chunk(b + 1, nxt - 2 * n_chunks)

    @pl.when(b == 0)
    def _():
        for c in range(DMA_SLOTS - 1):
            start_chunk(b, c)

    qb = q_ref[...]
    qrows = jnp.concatenate([qb] * N_HEADS, axis=0)
    rh = lax.broadcasted_iota(jnp.int32, (nrow, BRANCH_W), 0) >> (sq.bit_length() - 1)
    chh = lax.broadcasted_iota(jnp.int32, (nrow, BRANCH_W), 1) >> (HEAD_DIM.bit_length() - 1)
    head_mask = rh == chh
    qrows = jnp.where(head_mask, qrows, 0.0)
    qs = (qrows * (HEAD_DIM ** -0.5)).astype(BF16)

    def block_of(slot, n):
        return jnp.concatenate([buf[slot, pg] for pg in range(n * ppb, (n + 1) * ppb)], axis=1)

    lane_i = lax.broadcasted_iota(jnp.int32, (nrow, LANES), 1)
    gs = jnp.zeros((nrow, LANES), F32)
    for c in range(n_chunks):
        prefetch(c)
        wait_chunk(b, c)
        for n in range(bpc):
            blk_id = c * bpc + n
            s = jnp.dot(qs, block_of(c % DMA_SLOTS, n).astype(BF16), preferred_element_type=F32)
            s_sc[blk_id] = s
            gs = jnp.where(lane_i == blk_id, jnp.sum(s, axis=1, keepdims=True), gs)

    lane = lane_i.astype(F32)
    sel_f = jnp.zeros(gs.shape, F32)
    work = jnp.where(lane_i < nblk, gs, -jnp.inf)
    for _ in range(min(MOBA_TOPK, nblk)):
        mx = jnp.max(work, axis=1, keepdims=True)
        first = jnp.min(jnp.where(work == mx, lane, float(nblk)), axis=1, keepdims=True)
        pick = lane == first
        sel_f = jnp.where(pick, 1.0, sel_f)
        work = jnp.where(pick, -jnp.inf, work)

    s_own = lax.dot_general(qs, kn_ref[...].astype(BF16), _NT, preferred_element_type=F32)
    qi = lax.broadcasted_iota(jnp.int32, (nrow, sq), 0) & (sq - 1)
    kk = lax.broadcasted_iota(jnp.int32, (nrow, sq), 1)
    s_own = jnp.where(kk <= qi, s_own, NEG_INF)
    def masked(blk_id):
        return jnp.where(sel_f[:, blk_id:blk_id + 1] > 0.5, s_sc[blk_id], NEG_INF)

    m_run = masked(0)
    for n in range(1, nblk):
        m_run = jnp.maximum(m_run, masked(n))
    m = jnp.maximum(jnp.max(s_own, axis=1, keepdims=True), jnp.max(m_run, axis=1, keepdims=True))
    p_own = jnp.exp(s_own - m)
    acc = jnp.dot(p_own.astype(BF16), vn_ref[...].astype(BF16), preferred_element_type=F32)

    l_run = jnp.zeros((nrow, MOBA_BLOCK), F32)
    for c in range(n_chunks, 2 * n_chunks):
        prefetch(c)
        wait_chunk(b, c)
        for n in range(bpc):
            p = jnp.exp(masked((c - n_chunks) * bpc + n) - m)
            l_run = l_run + p
            acc = acc + lax.dot_general(p.astype(BF16), block_of(c % DMA_SLOTS, n).astype(BF16), _NT,
                                        preferred_element_type=F32)
    l = jnp.sum(p_own, axis=1, keepdims=True) + jnp.sum(l_run, axis=1, keepdims=True)

    full = jnp.where(head_mask, acc / l, 0.0)
    out = full[0:sq]
    for h in range(1, N_HEADS):
        out = out + full[h * sq:(h + 1) * sq]
    o_ref[...] = out.astype(o_ref.dtype)


def _moba_sample(q, k_new, v_new, cache_k, cache_v, page_table, layer, n_batch, sq):
    n_pages = page_table.shape[1]
    assert (n_pages * PAGE_SIZE) % MOBA_BLOCK == 0, "own block must hold only the new rows"
    assert n_pages % CHUNK_PAGES == 0 and (2 * n_pages // CHUNK_PAGES) % DMA_SLOTS == 0
    nblk = n_pages * PAGE_SIZE // MOBA_BLOCK
    assert nblk <= LANES, "block scores are kept one lane per block"
    depth, n_pool = cache_k.shape[0], cache_k.shape[1]
    ck = cache_k.transpose(0, 1, 3, 4, 2).reshape(depth, n_pool, BRANCH_W, PAGE_SIZE)
    cv = cache_v.transpose(0, 1, 3, 4, 2).reshape(depth, n_pool, BRANCH_W, PAGE_SIZE)
    q3, k3, v3 = (t.reshape(n_batch, sq, BRANCH_W) for t in (q, k_new, v_new))
    row = pl.BlockSpec((None, sq, BRANCH_W), lambda b, pt: (b, 0, 0))
    hbm = pl.BlockSpec(memory_space=pl.ANY)
    out = pl.pallas_call(
        functools.partial(_moba_sample_kernel, layer=layer, n_batch=n_batch, n_pages=n_pages, sq=sq),
        grid_spec=pltpu.PrefetchScalarGridSpec(
            num_scalar_prefetch=1,
            grid=(n_batch,),
            in_specs=[row, row, row, hbm, hbm],
            out_specs=row,
            scratch_shapes=[pltpu.VMEM((DMA_SLOTS, CHUNK_PAGES, BRANCH_W, PAGE_SIZE), F32),
                            pltpu.SemaphoreType.DMA((DMA_SLOTS,)),
                            pltpu.VMEM((nblk, N_HEADS * sq, MOBA_BLOCK), F32)]),
        out_shape=jax.ShapeDtypeStruct((n_batch, sq, BRANCH_W), BF16),
        compiler_params=_cparams(1),
        name="moba_sample",
    )(page_table, q3, k3, v3, ck, cv)
    return out.reshape(n_batch * sq, BRANCH_W)


def _merge_kernel(x_ref, a_ref, b_ref, c_ref, gpre_ref, gpost_ref, wg_ref, proj_ref, wo_ref, o_ref):
    x = x_ref[...]
    u = _rms(x, gpre_ref[...]).astype(BF16)
    gates = jax.nn.sigmoid(jnp.dot(u, wg_ref[...], preferred_element_type=F32))
    merged = None
    for n, br in enumerate((a_ref, b_ref, c_ref)):
        t = gates[:, n * D_MODEL:(n + 1) * D_MODEL] * jnp.dot(br[...], proj_ref[n], preferred_element_type=F32)
        merged = t if merged is None else merged + t
    m = jnp.dot(merged.astype(BF16), wo_ref[...], preferred_element_type=F32)
    o_ref[...] = x + _rms(m, gpost_ref[...])


def _merge(x, a, b, c, gpre, gpost, wg, proj, wo, tm):
    n = x.shape[0]
    row = lambda w: pl.BlockSpec((tm, w), lambda i: (i, 0))
    consts = (gpre, gpost, wg, proj, wo)
    return pl.pallas_call(
        _merge_kernel,
        grid=(n // tm,),
        in_specs=[row(D_MODEL)] + [row(BRANCH_W)] * 3 + [_const_spec(t.shape) for t in consts],
        out_specs=row(D_MODEL),
        out_shape=jax.ShapeDtypeStruct((n, D_MODEL), F32),
        compiler_params=_cparams(1),
        name="merge",
    )(x, a, b, c, *consts)


def _ffn_dense_kernel(x_ref, gpre_ref, gpost_ref, wg_ref, wu_ref, wd_ref, o_ref, u_sc, acc_sc):
    f = pl.program_id(1)

    @pl.when(f == 0)
    def _():
        u_sc[...] = _rms(x_ref[...], gpre_ref[...]).astype(BF16)
        acc_sc[...] = jnp.zeros(acc_sc.shape, F32)

    u = u_sc[...]
    hg = jnp.dot(u, wg_ref[...], preferred_element_type=F32)
    hu = jnp.dot(u, wu_ref[...], preferred_element_type=F32)
    acc_sc[...] += jnp.dot((jax.nn.silu(hg) * hu).astype(BF16), wd_ref[...], preferred_element_type=F32)

    @pl.when(f == pl.num_programs(1) - 1)
    def _():
        o_ref[...] = x_ref[...] + _rms(acc_sc[...], gpost_ref[...])


def _ffn_dense(x, gpre, gpost, wg, wu, wd, tm, fc):
    n = x.shape[0]
    dff = wg.shape[1]
    row = pl.BlockSpec((tm, D_MODEL), lambda i, f: (i, 0))
    mode = dict(pipeline_mode=pl.Buffered(1)) if fc == dff else {}
    return pl.pallas_call(
        _ffn_dense_kernel,
        grid=(n // tm, dff // fc),
        in_specs=[row, _const_spec(gpre.shape), _const_spec(gpost.shape),
                  pl.BlockSpec((D_MODEL, fc), lambda i, f: (0, f), **mode),
                  pl.BlockSpec((D_MODEL, fc), lambda i, f: (0, f), **mode),
                  pl.BlockSpec((fc, D_MODEL), lambda i, f: (f, 0), **mode)],
        out_specs=row,
        out_shape=jax.ShapeDtypeStruct((n, D_MODEL), F32),
        scratch_shapes=[pltpu.VMEM((tm, D_MODEL), BF16), pltpu.VMEM((tm, D_MODEL), F32)],
        compiler_params=_cparams(2),
        name="ffn_dense",
    )(x, gpre, gpost, wg, wu, wd)


def _router_gate(u, router_ref):
    logits = _dot_split(u, router_ref[...], _NN)
    lane = lax.broadcasted_iota(jnp.int32, logits.shape, 1)
    logits = jnp.where(lane < N_EXPERTS, logits, -jnp.inf)
    m1 = jnp.max(logits, axis=1, keepdims=True)
    i1 = jnp.min(jnp.where(logits == m1, lane, LANES), axis=1, keepdims=True)
    rest = jnp.where(lane == i1, -jnp.inf, logits)
    m2 = jnp.max(rest, axis=1, keepdims=True)
    i2 = jnp.min(jnp.where(rest == m2, lane, LANES), axis=1, keepdims=True)
    e2 = jnp.exp(m2 - m1)
    w1 = 1.0 / (1.0 + e2)
    w2 = e2 / (1.0 + e2)
    gate = jnp.where(lane == i1, w1, 0.0) + jnp.where(lane == i2, w2, 0.0)
    chosen = jnp.where((lane == i1) | (lane == i2), 1.0, 0.0)
    return gate, chosen


MOE_SLAB = 256


def _moe_kernel(x_ref, gpre_ref, gpost_ref, router_ref, *rest, tm, ck, sck, n_cff, n_cout):
    wg_refs, wu_refs = rest[:n_cff], rest[n_cff:2 * n_cff]
    wd_refs = rest[2 * n_cff:2 * n_cff + n_cout]
    o_ref, u_sc, gate_sc, rank_sc, rankt_sc, cnt_sc, xs_sc, y_sc = rest[2 * n_cff + n_cout:]
    e = pl.program_id(1)
    f = pl.program_id(2)
    nf = pl.num_programs(2)

    @pl.when((e == 0) & (f == 0))
    def _():
        u = _rms(x_ref[...], gpre_ref[...])
        u_sc[...] = u.astype(BF16)
        gate, chosen = _router_gate(u, router_ref)
        gate_sc[...] = gate
        earlier = (lax.broadcasted_iota(jnp.int32, (tm, tm), 1) < lax.broadcasted_iota(jnp.int32, (tm, tm), 0))
        rank = jnp.dot(jnp.where(earlier, 1.0, 0.0).astype(BF16), chosen.astype(BF16), preferred_element_type=F32)
        rank = jnp.where(chosen > 0.5, rank, -1.0)
        rank_sc[...] = rank
        rankt_sc[...] = rank.T[0:SUBLANES, :]
        cnt_sc[...] = jnp.sum(chosen, axis=0, keepdims=True)
        o_ref[...] = jnp.zeros(o_ref.shape, F32)
        y_sc[...] = jnp.zeros(y_sc.shape, F32)

    lane1 = lax.broadcasted_iota(jnp.int32, (1, LANES), 1)
    cnt = jnp.sum(jnp.where(lane1 == e, cnt_sc[...], 0.0))
    n_chunks = jnp.ceil(cnt * (1.0 / ck)).astype(jnp.int32)

    @pl.when(f == 0)
    def _():
        rank_row = rankt_sc[pl.ds(e, 1), :]
        slot = lax.broadcasted_iota(jnp.int32, (ck, 1), 0).astype(F32)

        def compact(c, carry):
            base = pl.multiple_of(c * ck, ck)
            onehot = jnp.where(rank_row - (c * ck).astype(F32) == slot, 1.0, 0.0).astype(BF16)
            xs_sc[pl.ds(base, ck), :] = jnp.dot(onehot, u_sc[...], preferred_element_type=F32).astype(BF16)
            return carry

        lax.fori_loop(0, n_chunks, compact, 0)

    def expert(c, carry):
        base = pl.multiple_of(c * ck, ck)
        xs = xs_sc[pl.ds(base, ck), :]
        act = jnp.concatenate(
            [(jax.nn.silu(jnp.dot(xs, g[...], preferred_element_type=F32))
              * jnp.dot(xs, u[...], preferred_element_type=F32)).astype(BF16)
             for g, u in zip(wg_refs, wu_refs)], axis=1)
        y = jnp.concatenate([jnp.dot(act, d[...], preferred_element_type=F32) for d in wd_refs], axis=1)

        @pl.when(f == 0)
        def _():
            y_sc[pl.ds(base, ck), :] = y

        @pl.when(f > 0)
        def _():
            y_sc[pl.ds(base, ck), :] += y

        return carry

    lax.fori_loop(0, n_chunks, expert, 0)

    @pl.when(f == nf - 1)
    def _():
        lane = lax.broadcasted_iota(jnp.int32, (MOE_SLAB, LANES), 1)
        slot = lax.broadcasted_iota(jnp.int32, (1, sck), 1).astype(F32)

        def scatter(c, carry):
            base = pl.multiple_of(c * sck, sck)
            yb = y_sc[pl.ds(base, sck), :].astype(BF16)
            for s0 in range(0, tm, MOE_SLAB):
                rows = slice(s0, s0 + MOE_SLAB)
                rank_col = jnp.sum(jnp.where(lane == e, rank_sc[rows, :], 0.0), axis=1, keepdims=True)
                gate_col = jnp.sum(jnp.where(lane == e, gate_sc[rows, :], 0.0), axis=1, keepdims=True)
                onehot_t = jnp.where(rank_col - (c * sck).astype(F32) == slot, 1.0, 0.0).astype(BF16)
                o_ref[rows, :] += gate_col * jnp.dot(onehot_t, yb, preferred_element_type=F32)
            return carry

        lax.fori_loop(0, (n_chunks * ck + sck - 1) // sck, scatter, 0)

    @pl.when((e == pl.num_programs(1) - 1) & (f == nf - 1))
    def _():
        o_ref[...] = x_ref[...] + _rms(o_ref[...], gpost_ref[...])


def _moe(x, gpre, gpost, router_pad, wg, wu, wd, tm, fc, ck):
    n = x.shape[0]
    n_exp, _, dff = wg.shape
    sck = 2 * ck if 2 * ck <= MXU_DEPTH else ck
    assert n_exp <= SUBLANES and tm % MOE_SLAB == 0 and ck % BF16_SUBLANES == 0
    cap = pl.cdiv(tm, sck) * sck
    row = pl.BlockSpec((tm, D_MODEL), lambda i, e, f: (i, 0))
    assert D_MODEL % MXU_DEPTH == 0 and fc % MXU_DEPTH == 0
    n_cff, n_cout = fc // MXU_DEPTH, D_MODEL // MXU_DEPTH
    ff_slabs = [pl.BlockSpec((None, D_MODEL, MXU_DEPTH), lambda i, e, f, k=k: (e, 0, f * n_cff + k))
                for k in range(n_cff)]
    out_slabs = [pl.BlockSpec((None, fc, MXU_DEPTH), lambda i, e, f, k=k: (e, f, k)) for k in range(n_cout)]
    return pl.pallas_call(
        functools.partial(_moe_kernel, tm=tm, ck=ck, sck=sck, n_cff=n_cff, n_cout=n_cout),
        grid=(n // tm, n_exp, dff // fc),
        in_specs=[row, _const_spec(gpre.shape), _const_spec(gpost.shape), _const_spec(router_pad.shape)]
        + ff_slabs + ff_slabs + out_slabs,
        out_specs=row,
        out_shape=jax.ShapeDtypeStruct((n, D_MODEL), F32),
        scratch_shapes=[pltpu.VMEM((tm, D_MODEL), BF16), pltpu.VMEM((tm, LANES), F32),
                        pltpu.VMEM((tm, LANES), F32), pltpu.VMEM((SUBLANES, tm), F32),
                        pltpu.VMEM((1, LANES), F32),
                        pltpu.VMEM((cap, D_MODEL), BF16), pltpu.VMEM((cap, D_MODEL), F32)],
        compiler_params=_cparams(3),
        name="moe",
    )(x, gpre, gpost, router_pad, *([wg] * n_cff), *([wu] * n_cff), *([wd] * n_cout))


def _block_diag_halves(w_a, w_i):
    half_blocks = LRU_BLOCKS // 2

    def bd(w4):
        eye = jnp.eye(half_blocks, dtype=w4.dtype)
        return jnp.einsum('hcd,hg->hcgd', w4, eye).reshape(half_blocks * LRU_BLOCK_W, half_blocks * LRU_BLOCK_W)

    halves = [jnp.concatenate([bd(w_a[i * half_blocks:(i + 1) * half_blocks]),
                               bd(w_i[i * half_blocks:(i + 1) * half_blocks])], axis=1) for i in range(2)]
    return jnp.stack(halves).astype(BF16)


def _row(v):
    return v.reshape(1, -1)


def _pad_hist(h, pad_rows):
    return jnp.pad(h, ((0, 0), (pad_rows - h.shape[1], 0), (0, 0)))


PROMPT_TS = 512
ROW_TM = 512
FFN_FC = 2816
MOE_FC = 1792
MOE_TM = 1024
MOE_CHUNK = 288
MOE_CHUNK_SMALL = 128


def kernel(x_prompt, x_sample, cache_k, cache_v, page_table, state_pool, state_conv, state_h, norm_mix_pre, norm_mix_post, norm_ffn_pre, norm_ffn_post, w_in, pool_w, pool_scale, conv_w, conv_b, lru_wa, lru_ba, lru_wi, lru_bi, lru_lambda, proj_pool, proj_attn, proj_lru, w_o, ffn_w_gate, ffn_w_up, ffn_w_down, router_w, moe_w_gate, moe_w_up, moe_w_down):
    depth = w_in.shape[0]
    bp, sp, _ = x_prompt.shape
    bs, ss, _ = x_sample.shape
    past_len = page_table.shape[1] * PAGE_SIZE
    xp = x_prompt.reshape(bp * sp, D_MODEL)
    xs = x_sample.reshape(bs * ss, D_MODEL)
    tm_s = bs * ss

    outs = {name: [] for name in ("ks", "vs", "pp", "ps", "cp", "cs", "hp", "hs")}
    kv_pages = tuple(jnp.zeros((depth, bp, sp // PAGE_SIZE, BRANCH_W, PAGE_SIZE), F32) for _ in range(2))
    for l in range(depth):
        w_qkv = w_in[l, :, :D_QKV].astype(BF16)
        w_gates = w_in[l, :, D_QKV:].astype(BF16)
        mix_consts = (_row(norm_mix_pre[l]), w_qkv, pool_w[l].astype(BF16), _row(pool_scale[l]),
                      conv_w[l], _row(conv_b[l]), _block_diag_halves(lru_wa[l], lru_wi[l]),
                      _row(lru_ba[l]), _row(lru_bi[l]), _row(lru_lambda[l]))
        proj = jnp.stack([proj_pool[l], proj_attn[l], proj_lru[l]]).astype(BF16)
        merge_consts = (_row(norm_mix_pre[l]), _row(norm_mix_post[l]), w_gates, proj, w_o[l].astype(BF16))

        qt_p, kbf_p, vtb_p, kmean_p, kpages, vpages, a_p, c_p, pt_p, ct_p, h_p = _mixer_in_prompt(
            xp, bp, sp, PROMPT_TS, l, depth, kv_pages, *mix_consts)
        kv_pages = (kpages, vpages)
        b_p = _moba_prompt(qt_p, kbf_p, vtb_p, kmean_p.reshape(bp, sp // MOBA_BLOCK, BRANCH_W), bp, sp)
        xp = _merge(xp, a_p, b_p, c_p, *merge_consts, ROW_TM)

        q_s, k_s, v_s, a_s, c_s, pt_s, ct_s, h_s = _mixer_in_sample(
            xs, bs, ss, past_len, _pad_hist(state_pool[l], POOL_PAD), _pad_hist(state_conv[l], CONV_PAD),
            state_h[l].reshape(bs, 1, BRANCH_W), *mix_consts)
        b_s = _moba_sample(q_s, k_s, v_s, cache_k, cache_v, page_table, l, bs, ss)
        xs = _merge(xs, a_s, b_s, c_s, *merge_consts, tm_s)

        gpre, gpost = _row(norm_ffn_pre[l]), _row(norm_ffn_post[l])
        j = l // 2
        if l % 2 == 0:
            wg, wu, wd = (w[j].astype(BF16) for w in (ffn_w_gate, ffn_w_up, ffn_w_down))
            xp = _ffn_dense(xp, gpre, gpost, wg, wu, wd, ROW_TM, FFN_FC)
            xs = _ffn_dense(xs, gpre, gpost, wg, wu, wd, tm_s, FFN_FC)
        else:
            wg, wu, wd = (w[j].astype(BF16) for w in (moe_w_gate, moe_w_up, moe_w_down))
            router_pad = jnp.pad(router_w[j], ((0, 0), (0, LANES - N_EXPERTS)))
            xp = _moe(xp, gpre, gpost, router_pad, wg, wu, wd, MOE_TM, MOE_FC, MOE_CHUNK)
            xs = _moe(xs, gpre, gpost, router_pad, wg, wu, wd, tm_s, MOE_FC, MOE_CHUNK_SMALL)

        outs["ks"].append(k_s.reshape(bs, ss, N_HEADS, HEAD_DIM))
        outs["vs"].append(v_s.reshape(bs, ss, N_HEADS, HEAD_DIM))
        outs["pp"].append(pt_p[:, POOL_PAD - POOL_HIST:])
        outs["ps"].append(pt_s[:, POOL_PAD - POOL_HIST:])
        outs["cp"].append(ct_p[:, CONV_PAD - (CONV_W - 1):])
        outs["cs"].append(ct_s[:, CONV_PAD - (CONV_W - 1):])
        outs["hp"].append(h_p.reshape(bp, BRANCH_W))
        outs["hs"].append(h_s.reshape(bs, BRANCH_W))

    st = {k: jnp.stack(v) for k, v in outs.items()}
    kp, vp = (t.reshape(depth, bp, sp // PAGE_SIZE, N_HEADS, HEAD_DIM, PAGE_SIZE).transpose(0, 1, 2, 5, 3, 4)
              for t in kv_pages)
    return (xp.reshape(bp, sp, D_MODEL), xs.reshape(bs, ss, D_MODEL),
            kp, vp, st["ks"], st["vs"], st["pp"], st["ps"], st["cp"], st["cs"], st["hp"], st["hs"])
```
